```python
import jax, jax.numpy as jnp
from jax import lax
import numpy as np

D_MODEL = 1024
BATCH = 4
SEQ = 8192
DEPTH = 1
DEC_BATCH = 16
DEC_SEQ = 16
PAST_LEN = 1024

CHUNK = 64
D_CONV = D_MODEL // 2
CONV_W = 3
HG_HEADS = 4
HG_KDIM = 128
HG_VDIM = 128
HG_W = HG_HEADS * HG_KDIM
N_MEM = 256
X_HEADS = 4
X_HEAD_DIM = D_MODEL // X_HEADS
D_FF = 4 * D_MODEL
N_BRANCH = 2
IN_COLS = 3 * D_CONV + 4 * HG_W + N_BRANCH * D_MODEL
EPS = 1e-6

kernel_name = "hybrid_conv_hgrn2_memxattn_step"

F32 = jnp.float32


def _rmsnorm(x, g):
    xf = x.astype(F32)
    y = xf * lax.rsqrt(jnp.mean(xf * xf, axis=-1, keepdims=True) + EPS)
    return (y * g.astype(F32)).astype(x.dtype)


def _causal_conv(u, prefix, w):
    T = u.shape[1]
    padded = jnp.concatenate([prefix.astype(u.dtype), u], axis=1)
    y = padded[:, 0:T] * w[0]
    for j in range(1, CONV_W):
        y = y + padded[:, j:j + T] * w[j]
    return y, padded[:, T:]


def _hgrn_block(q, k, v, logf, S0):
    L = q.shape[2]
    b = jnp.cumsum(logf, axis=2)
    causal = jnp.tril(jnp.ones((L, L), dtype=bool))
    diff = b[:, :, :, None, :] - b[:, :, None, :, :]
    decay = jnp.exp(jnp.where(causal[None, None, :, :, None], diff, -jnp.inf))
    scores = jnp.einsum('bhtk,bhsk,bhtsk->bhts', q, k, decay)
    o = jnp.einsum('bhts,bhsv->bhtv', scores, v) + jnp.einsum('bhtk,bhkv->bhtv', q * jnp.exp(b), S0)
    b_last = b[:, :, -1:, :]
    S = jnp.exp(b_last[:, :, 0, :])[..., None] * S0 + jnp.einsum('bhsk,bhsv->bhkv', k * jnp.exp(b_last - b), v)
    return o, S


def _hgrn_recurrence(q, k, v, logf, S0):
    B, H, T, K = q.shape
    if T <= CHUNK:
        return _hgrn_block(q, k, v, logf, S0)
    nc = T // CHUNK

    def to_blocks(a):
        return a.reshape(B, H, nc, CHUNK, a.shape[-1]).transpose(2, 0, 1, 3, 4)

    def step(S, blk):
        qb, kb, vb, fb = blk
        o, S = _hgrn_block(qb, kb, vb, fb, S)
        return S, o

    S, o = lax.scan(step, S0, (to_blocks(q), to_blocks(k), to_blocks(v), to_blocks(logf)))
    o = o.transpose(1, 2, 0, 3, 4).reshape(B, H, T, v.shape[-1])
    return o, S


def _mixer(h, conv_prefix, S0, w_in, conv_w, lb, hg_norm, w_conv_out, w_hg_out, w_o):
    Bn, T, _ = h.shape
    splits = (D_CONV, 2 * D_CONV, 3 * D_CONV,
              3 * D_CONV + HG_W, 3 * D_CONV + 2 * HG_W, 3 * D_CONV + 3 * HG_W, 3 * D_CONV + 4 * HG_W,
              3 * D_CONV + 4 * HG_W + D_MODEL)
    proj = h @ w_in
    cb, cc, cx, hq, hf, hi, hg, ga, gb = jnp.split(proj, splits, axis=-1)
    y_conv, conv_state = _causal_conv(cc * cx, conv_prefix, conv_w)
    y_a = (cb * y_conv) @ w_conv_out
    def heads(a):
        return a.astype(F32).reshape(Bn, T, HG_HEADS, -1).transpose(0, 2, 1, 3)
    q = jax.nn.silu(heads(hq))
    lbh = lb.astype(F32).reshape(1, HG_HEADS, 1, HG_KDIM)
    f = lbh + (1.0 - lbh) * jax.nn.sigmoid(heads(hf))
    logf = jnp.log(f)
    k = 1.0 - f
    v = heads(hi)
    o, S = _hgrn_recurrence(q, k, v, logf, S0.astype(F32))
    o = o.transpose(0, 2, 1, 3)
    o = o * lax.rsqrt(jnp.mean(o * o, axis=-1, keepdims=True) + EPS) * hg_norm.astype(F32)
    o = o * jax.nn.silu(hg.astype(F32).reshape(Bn, T, HG_HEADS, HG_VDIM))
    y_b = o.reshape(Bn, T, HG_W).astype(h.dtype) @ w_hg_out
    merged = jax.nn.sigmoid(ga) * y_a + jax.nn.sigmoid(gb) * y_b
    return merged @ w_o, conv_state, S


def _mem_kv(mem, g, w_xk, w_xv):
    Bn = mem.shape[0]
    mh = _rmsnorm(mem, g)
    mk = (mh @ w_xk).reshape(Bn, N_MEM, X_HEADS, X_HEAD_DIM)
    mv = (mh @ w_xv).reshape(Bn, N_MEM, X_HEADS, X_HEAD_DIM)
    return mk, mv


def _cross_attn(h, mk, mv, w_xq, w_xo):
    Bn, T, _ = h.shape
    q = (h @ w_xq).reshape(Bn, T, X_HEADS, X_HEAD_DIM)
    s = jnp.einsum('bthd,bmhd->bhtm', q, mk.astype(q.dtype)).astype(F32) * (X_HEAD_DIM ** -0.5)
    p = jax.nn.softmax(s, axis=-1).astype(h.dtype)
    o = jnp.einsum('bhtm,bmhd->bthd', p, mv.astype(h.dtype)).reshape(Bn, T, D_MODEL)
    return o @ w_xo


def _ffn(h, w_up, w_down):
    a = jax.nn.relu(h @ w_up)
    return (a * a) @ w_down


def _layer(x, conv_prefix, S0, mk, mv, lb, norm_mix, w_in, conv_w, hg_norm, w_conv_out, w_hg_out, w_o,
           norm_x, w_xq, w_xo, norm_ffn, w_up, w_down):
    m, conv_state, S = _mixer(_rmsnorm(x, norm_mix), conv_prefix, S0, w_in, conv_w, lb, hg_norm,
                              w_conv_out, w_hg_out, w_o)
    x = x + m
    x = x + _cross_attn(_rmsnorm(x, norm_x), mk, mv, w_xq, w_xo)
    x = x + _ffn(_rmsnorm(x, norm_ffn), w_up, w_down)
    return x, conv_state, S


def setup_inputs(seed: int = 0) -> dict:
    key = jax.random.key(seed)
    ks = jax.random.split(key, 32)
    nrm = lambda k, shape, s: jax.random.normal(k, shape, F32) * s
    gain = lambda k, shape: 1.0 + 0.05 * jax.random.normal(k, shape, F32)
    return {
        "x_prompt": nrm(ks[0], (BATCH, SEQ, D_MODEL), 1.0),
        "x_sample": nrm(ks[1], (DEC_BATCH, DEC_SEQ, D_MODEL), 1.0),
        "mem_prompt": nrm(ks[2], (BATCH, N_MEM, D_MODEL), 1.0),
        "state_conv": nrm(ks[3], (DEPTH, DEC_BATCH, CONV_W - 1, D_CONV), 1.0),
        "state_hgrn": nrm(ks[4], (DEPTH, DEC_BATCH, HG_HEADS, HG_KDIM, HG_VDIM), 0.5),
        "cache_mem_k": nrm(ks[5], (DEPTH, DEC_BATCH, N_MEM, X_HEADS, X_HEAD_DIM), 1.0),
        "cache_mem_v": nrm(ks[6], (DEPTH, DEC_BATCH, N_MEM, X_HEADS, X_HEAD_DIM), 1.0),
        "norm_mix": gain(ks[7], (DEPTH, D_MODEL)),
        "w_in": nrm(ks[8], (DEPTH, D_MODEL, IN_COLS), D_MODEL ** -0.5),
        "conv_w": nrm(ks[9], (DEPTH, CONV_W, D_CONV), CONV_W ** -0.5),
        "hg_lb": nrm(ks[10], (DEPTH + 1, HG_W), 0.5),
        "hg_norm": gain(ks[11], (DEPTH, HG_VDIM)),
        "w_conv_out": nrm(ks[12], (DEPTH, D_CONV, D_MODEL), D_CONV ** -0.5),
        "w_hg_out": nrm(ks[13], (DEPTH, HG_W, D_MODEL), HG_W ** -0.5),
        "w_o": nrm(ks[14], (DEPTH, D_MODEL, D_MODEL), D_MODEL ** -0.5),
        "norm_x": gain(ks[15], (DEPTH, D_MODEL)),
        "norm_mem": gain(ks[16], (DEPTH, D_MODEL)),
        "w_xq": nrm(ks[17], (DEPTH, D_MODEL, D_MODEL), D_MODEL ** -0.5),
        "w_xk": nrm(ks[18], (DEPTH, D_MODEL, D_MODEL), D_MODEL ** -0.5),
        "w_xv": nrm(ks[19], (DEPTH, D_MODEL, D_MODEL), D_MODEL ** -0.5),
        "w_xo": nrm(ks[20], (DEPTH, D_MODEL, D_MODEL), D_MODEL ** -0.5),
        "norm_ffn": gain(ks[21], (DEPTH, D_MODEL)),
        "w_up": nrm(ks[22], (DEPTH, D_MODEL, D_FF), D_MODEL ** -0.5),
        "w_down": nrm(ks[23], (DEPTH, D_FF, D_MODEL), D_FF ** -0.5),
        "norm_final": gain(ks[24], (D_MODEL,)),
    }


def reference(x_prompt, x_sample, mem_prompt, state_conv, state_hgrn, cache_mem_k, cache_mem_v,
              norm_mix, w_in, conv_w, hg_lb, hg_norm, w_conv_out, w_hg_out, w_o,
              norm_x, norm_mem, w_xq, w_xk, w_xv, w_xo, norm_ffn, w_up, w_down, norm_final):
    lb_all = jnp.cumsum(jax.nn.softmax(hg_lb.astype(F32), axis=0), axis=0)
    xp, xs = x_prompt, x_sample
    Bp = x_prompt.shape[0]
    conv_p, hg_p, mk_p_l, mv_p_l, conv_s, hg_s = [], [], [], [], [], []
    for l in range(DEPTH):
        shared = dict(lb=lb_all[l], norm_mix=norm_mix[l], w_in=w_in[l], conv_w=conv_w[l], hg_norm=hg_norm[l],
                      w_conv_out=w_conv_out[l], w_hg_out=w_hg_out[l], w_o=w_o[l], norm_x=norm_x[l],
                      w_xq=w_xq[l], w_xo=w_xo[l], norm_ffn=norm_ffn[l], w_up=w_up[l], w_down=w_down[l])
        mk_p, mv_p = _mem_kv(mem_prompt, norm_mem[l], w_xk[l], w_xv[l])
        prefix0 = jnp.zeros((Bp, CONV_W - 1, D_CONV), xp.dtype)
        S0 = jnp.zeros((Bp, HG_HEADS, HG_KDIM, HG_VDIM), F32)
        xp, cst_p, S_p = _layer(xp, prefix0, S0, mk_p, mv_p, **shared)
        conv_p.append(cst_p.astype(x_prompt.dtype))
        hg_p.append(S_p.astype(x_prompt.dtype))
        mk_p_l.append(mk_p)
        mv_p_l.append(mv_p)
        xs, cst_s, S_s = _layer(xs, state_conv[l], state_hgrn[l], cache_mem_k[l], cache_mem_v[l], **shared)
        conv_s.append(cst_s.astype(x_sample.dtype))
        hg_s.append(S_s.astype(x_sample.dtype))
    y_prompt = _rmsnorm(xp, norm_final)
    y_sample = _rmsnorm(xs, norm_final)
    return (y_prompt, y_sample, jnp.stack(conv_p), jnp.stack(hg_p), jnp.stack(mk_p_l), jnp.stack(mv_p_l),
            jnp.stack(conv_s), jnp.stack(hg_s))
```

```python
import functools

import jax
import jax.numpy as jnp
from jax import lax
from jax.experimental import pallas as pl
from jax.experimental.pallas import tpu as pltpu

F32 = jnp.float32
BF16 = jnp.bfloat16
EPS = 1e-6

CONV_W = 3
HG_HEADS = 4
HG_DIM = 128
X_HEADS = 4
DECAY_BLOCK = 16
HG_CHUNK = 64
CONV_PAD = 8
FF_TILE = 1024

V7X_VMEM_BYTES = 64 * 1024 * 1024
VMEM_LIMIT_BYTES = V7X_VMEM_BYTES - 8 * 1024 * 1024

_NT = (((1,), (1,)), ((), ()))
_TN = (((0,), (0,)), ((), ()))


def _rms(x, g):
    return x * lax.rsqrt(jnp.mean(x * x, axis=-1, keepdims=True) + EPS) * g


def _dot(a, b):
    return jnp.dot(a, b, preferred_element_type=F32)


def _cross_block_pieces(lo, hi):
    if hi - lo <= 1:
        return []
    mid = (lo + hi) // 2
    return ([(i, list(range(lo, mid))) for i in range(mid, hi)]
            + _cross_block_pieces(lo, mid) + _cross_block_pieces(mid, hi))


def _pair_mask(chunk, pieces):
    n_cols = chunk + DECAY_BLOCK * sum(len(js) for _, js in pieces)
    t = lax.broadcasted_iota(jnp.int32, (chunk, n_cols), 0)
    c = lax.broadcasted_iota(jnp.int32, (chunk, n_cols), 1)
    blk_shift = DECAY_BLOCK.bit_length() - 1
    t_blk = t >> blk_shift
    mask = (c < chunk) & ((c >> blk_shift) == t_blk) & (c <= t)
    col = chunk
    for i, js in pieces:
        width = DECAY_BLOCK * len(js)
        mask = mask | ((c >= col) & (c < col + width) & (t_blk == i))
        col += width
    return mask


def _mixer_kernel(x_ref, pre_ref, s0_ref, hglb_ref, nmix_ref, win_ref, cw_ref, hgn_ref, wco_ref, who_ref, wo_ref,
                  x1_ref, cst_ref, sout_ref,
                  st_scr, ubuf, qp_scr, kn_scr, v_scr, a_scr, o_scr, *, layer, tm, chunk):
    d_conv = cw_ref.shape[1]
    hg_w = HG_HEADS * HG_DIM
    d_model = x_ref.shape[1]
    t_idx = pl.program_id(1)
    tail = CONV_W - 1

    @pl.when(t_idx == 0)
    def _load_state():
        for hh in range(HG_HEADS):
            st_scr[hh] = s0_ref[hh].T
        ubuf[CONV_PAD - tail:CONV_PAD, :] = pre_ref[...]

    x = x_ref[...]
    h = _rms(x, nmix_ref[...]).astype(BF16)

    pc = _dot(h, win_ref[:, 0:3 * d_conv])
    cb = pc[:, 0:d_conv]
    u = pc[:, d_conv:2 * d_conv] * pc[:, 2 * d_conv:3 * d_conv]
    ubuf[CONV_PAD:CONV_PAD + tm, :] = u
    cw = cw_ref[...]
    y_conv = ubuf[CONV_PAD - 2:CONV_PAD - 2 + tm, :] * cw[0:1]
    for j in range(1, CONV_W):
        y_conv = y_conv + ubuf[CONV_PAD - 2 + j:CONV_PAD - 2 + j + tm, :] * cw[j:j + 1]
    y_a = _dot((cb * y_conv).astype(BF16), wco_ref[...])
    u_tail = ubuf[CONV_PAD + tm - tail:CONV_PAD + tm, :]
    ubuf[CONV_PAD - tail:CONV_PAD, :] = u_tail
    cst_ref[...] = u_tail

    c0 = 3 * d_conv
    ph = _dot(h, win_ref[:, c0:c0 + 3 * hg_w])
    hq = ph[:, 0:hg_w]
    hf = ph[:, hg_w:2 * hg_w]
    e_lb = jnp.exp(hglb_ref[...] - jnp.max(hglb_ref[...], axis=0, keepdims=True))
    lb = jnp.sum(e_lb[0:layer + 1], axis=0, keepdims=True) / jnp.sum(e_lb, axis=0, keepdims=True)
    q = hq * jax.nn.sigmoid(hq)
    f = lb + (1.0 - lb) * jax.nn.sigmoid(hf)
    a = jnp.log(f)
    row_in_blk = lax.broadcasted_iota(jnp.int32, (tm, hg_w), 0) & (DECAY_BLOCK - 1)
    shift = 1
    while shift < DECAY_BLOCK:
        a = a + jnp.where(row_in_blk >= shift, pltpu.roll(a, shift, 0), 0.0)
        shift *= 2
    qp_scr[...] = q * jnp.exp(a)
    kn_scr[...] = (1.0 - f) * jnp.exp(-a)
    v_scr[...] = ph[:, 2 * hg_w:3 * hg_w].astype(BF16)
    a_scr[...] = a

    nblk = chunk // DECAY_BLOCK
    pieces = _cross_block_pieces(0, nblk)
    mask = _pair_mask(chunk, pieces)

    def blk(arr, j):
        return arr[j * DECAY_BLOCK:(j + 1) * DECAY_BLOCK, :]

    def chunk_body(c, carry):
        r0 = pl.multiple_of(c * chunk, chunk)
        qp = qp_scr[pl.ds(r0, chunk), :]
        kn = kn_scr[pl.ds(r0, chunk), :]
        v = v_scr[pl.ds(r0, chunk), :]
        cs = [jnp.zeros((1, hg_w), F32)]
        for j in range(nblk):
            cs.append(cs[j] + a_scr[pl.ds(r0 + (j + 1) * DECAY_BLOCK - 1, 1), :])
        k_parts, v_parts = [kn], [v]
        for i, js in pieces:
            for j in js:
                k_parts.append(blk(kn, j) * jnp.exp(cs[i] - cs[j]))
                v_parts.append(blk(v, j))
        k_cat = jnp.concatenate(k_parts, axis=0).astype(BF16)
        v_cat = jnp.concatenate(v_parts, axis=0)
        q_in = jnp.concatenate([blk(qp, j) if j == 0 else blk(qp, j) * jnp.exp(cs[j]) for j in range(nblk)],
                               axis=0).astype(BF16)
        k_st = jnp.concatenate([blk(kn, j) * jnp.exp(cs[nblk] - cs[j]) for j in range(nblk)], axis=0).astype(BF16)
        dec = jnp.exp(cs[nblk])
        qp_b = qp.astype(BF16)
        for hh in range(HG_HEADS):
            sl = slice(hh * HG_DIM, (hh + 1) * HG_DIM)
            st = st_scr[hh]
            r = lax.dot_general(qp_b[:, sl], k_cat[:, sl], _NT, preferred_element_type=F32)
            w = jnp.where(mask, r, 0.0).astype(BF16)
            o = _dot(w, v_cat[:, sl]) + lax.dot_general(q_in[:, sl], st.astype(BF16), _NT,
                                                        preferred_element_type=F32)
            o_scr[pl.ds(r0, chunk), sl] = o
            st_scr[hh] = dec[:, sl] * st + lax.dot_general(v[:, sl], k_st[:, sl], _TN, preferred_element_type=F32)
        return carry

    lax.fori_loop(0, tm // chunk, chunk_body, 0)

    pg = _dot(h, win_ref[:, c0 + 3 * hg_w:])
    hg = pg[:, 0:hg_w]
    g_a = pg[:, hg_w:hg_w + d_model]
    g_b = pg[:, hg_w + d_model:]
    o_all = o_scr[...]
    hgn = hgn_ref[...]
    normed = []
    for hh in range(HG_HEADS):
        normed.append(_rms(o_all[:, hh * HG_DIM:(hh + 1) * HG_DIM], hgn))
    o_n = jnp.concatenate(normed, axis=1) * (hg * jax.nn.sigmoid(hg))
    y_b = _dot(o_n.astype(BF16), who_ref[...])
    merged = jax.nn.sigmoid(g_a) * y_a + jax.nn.sigmoid(g_b) * y_b
    x1_ref[...] = x + _dot(merged.astype(BF16), wo_ref[...])

    @pl.when(t_idx == pl.num_programs(1) - 1)
    def _store_state():
        for hh in range(HG_HEADS):
            sout_ref[hh] = st_scr[hh].T


def _xffn_kernel(x_ref, mk_ref, mv_ref, nx_ref, wq_ref, wxo_ref, nf_ref, wup_ref, wdn_ref, nfin_ref,
                 y_ref, mk_scr, mv_scr, *, final_norm):
    d_model = x_ref.shape[1]
    d_ff = wup_ref.shape[1]
    xd = d_model // X_HEADS

    @pl.when(pl.program_id(1) == 0)
    def _load_memory():
        mk_scr[...] = mk_ref[...].astype(BF16)
        mv_scr[...] = mv_ref[...].astype(BF16)

    x = x_ref[...]
    h = _rms(x, nx_ref[...]).astype(BF16)
    q = _dot(h, wq_ref[...])
    heads = []
    for hh in range(X_HEADS):
        sl = slice(hh * xd, (hh + 1) * xd)
        s = lax.dot_general(q[:, sl].astype(BF16), mk_scr[:, sl], _NT, preferred_element_type=F32) * (xd ** -0.5)
        e = jnp.exp(s - jnp.max(s, axis=-1, keepdims=True))
        p = e / jnp.sum(e, axis=-1, keepdims=True)
        heads.append(_dot(p.astype(BF16), mv_scr[:, sl]))
    o = jnp.concatenate(heads, axis=1)
    x2 = x + _dot(o.astype(BF16), wxo_ref[...])

    h3 = _rms(x2, nf_ref[...]).astype(BF16)
    acc = None
    for j in range(d_ff // FF_TILE):
        act = jnp.maximum(_dot(h3, wup_ref[:, j * FF_TILE:(j + 1) * FF_TILE]), 0.0)
        part = _dot((act * act).astype(BF16), wdn_ref[j * FF_TILE:(j + 1) * FF_TILE, :])
        acc = part if acc is None else acc + part
    x3 = x2 + acc
    y_ref[...] = _rms(x3, nfin_ref[...]) if final_norm else x3


def _memkv_kernel(mem_ref, g_ref, wk_ref, wv_ref, mk_ref, mv_ref):
    mh = _rms(mem_ref[...], g_ref[...]).astype(BF16)
    mk_ref[...] = _dot(mh, wk_ref[...])
    mv_ref[...] = _dot(mh, wv_ref[...])


def _resident(shape):
    return pl.BlockSpec(shape, lambda *_: (0,) * len(shape), pipeline_mode=pl.Buffered(1))


def _token_tile(t_len):
    for tm in (512, 256, 128, 64, 32, 16):
        if t_len % tm == 0:
            return tm
    raise ValueError(f"sequence length {t_len} must be a multiple of {DECAY_BLOCK}")


def _params():
    return pltpu.CompilerParams(dimension_semantics=("arbitrary", "arbitrary"), vmem_limit_bytes=VMEM_LIMIT_BYTES)


def _mixer(x, prefix, s0, hg_lb, norm_mix, w_in, conv_w, hg_norm, w_conv_out, w_hg_out, w_o, *, layer):
    b, t_len, d = x.shape
    d_conv = conv_w.shape[1]
    hg_w = HG_HEADS * HG_DIM
    tm = _token_tile(t_len)
    chunk = min(HG_CHUNK, tm)
    tok = pl.BlockSpec((None, tm, d), lambda i, j: (i, j, 0))
    per_batch = lambda *shape: pl.BlockSpec((None,) + shape, lambda i, j: (i,) + (0,) * len(shape))
    return pl.pallas_call(
        functools.partial(_mixer_kernel, layer=layer, tm=tm, chunk=chunk),
        grid=(b, t_len // tm),
        in_specs=[tok, per_batch(CONV_W - 1, d_conv), per_batch(HG_HEADS, HG_DIM, HG_DIM),
                  _resident(hg_lb.shape), _resident((1, d)), _resident(w_in.shape), _resident(conv_w.shape),
                  _resident((1, HG_DIM)), _resident(w_conv_out.shape), _resident(w_hg_out.shape),
                  _resident(w_o.shape)],
        out_specs=[tok, per_batch(CONV_W - 1, d_conv), per_batch(HG_HEADS, HG_DIM, HG_DIM)],
        out_shape=[jax.ShapeDtypeStruct((b, t_len, d), F32),
                   jax.ShapeDtypeStruct((b, CONV_W - 1, d_conv), F32),
                   jax.ShapeDtypeStruct((b, HG_HEADS, HG_DIM, HG_DIM), F32)],
        scratch_shapes=[pltpu.VMEM((HG_HEADS, HG_DIM, HG_DIM), F32),
                        pltpu.VMEM((CONV_PAD + tm, d_conv), F32),
                        pltpu.VMEM((tm, hg_w), F32), pltpu.VMEM((tm, hg_w), F32),
                        pltpu.VMEM((tm, hg_w), BF16), pltpu.VMEM((tm, hg_w), F32),
                        pltpu.VMEM((tm, hg_w), F32)],
        compiler_params=_params(),
        name="mixer",
    )(x, prefix, s0, hg_lb, norm_mix.reshape(1, d), w_in, conv_w, hg_norm.reshape(1, HG_DIM),
      w_conv_out, w_hg_out, w_o)


def _xffn(x, mk, mv, norm_x, w_xq, w_xo, norm_ffn, w_up, w_down, norm_final, *, final_norm):
    b, t_len, d = x.shape
    n_mem = mk.shape[1]
    tm = _token_tile(t_len)
    tok = pl.BlockSpec((None, tm, d), lambda i, j: (i, j, 0))
    mem = pl.BlockSpec((None, n_mem, d), lambda i, j: (i, 0, 0))
    return pl.pallas_call(
        functools.partial(_xffn_kernel, final_norm=final_norm),
        grid=(b, t_len // tm),
        in_specs=[tok, mem, mem, _resident((1, d)), _resident(w_xq.shape), _resident(w_xo.shape),
                  _resident((1, d)), _resident(w_up.shape), _resident(w_down.shape), _resident((1, d))],
        out_specs=tok,
        out_shape=jax.ShapeDtypeStruct((b, t_len, d), F32),
        scratch_shapes=[pltpu.VMEM((n_mem, d), BF16), pltpu.VMEM((n_mem, d), BF16)],
        compiler_params=_params(),
        name="xffn",
    )(x, mk, mv, norm_x.reshape(1, d), w_xq, w_xo, norm_ffn.reshape(1, d), w_up, w_down, norm_final.reshape(1, d))


def _mem_kv(mem, g, w_xk, w_xv):
    b, n_mem, d = mem.shape
    blk = pl.BlockSpec((None, n_mem, d), lambda i: (i, 0, 0))
    return pl.pallas_call(
        _memkv_kernel,
        grid=(b,),
        in_specs=[blk, _resident((1, d)), _resident(w_xk.shape), _resident(w_xv.shape)],
        out_specs=[blk, blk],
        out_shape=[jax.ShapeDtypeStruct((b, n_mem, d), F32)] * 2,
        compiler_params=pltpu.CompilerParams(dimension_semantics=("arbitrary",), vmem_limit_bytes=VMEM_LIMIT_BYTES),
        name="mem_kv",
    )(mem, g.reshape(1, d), w_xk, w_xv)


def kernel(x_prompt, x_sample, mem_prompt, state_conv, state_hgrn, cache_mem_k, cache_mem_v, norm_mix, w_in, conv_w, hg_lb, hg_norm, w_conv_out, w_hg_out, w_o, norm_x, norm_mem, w_xq, w_xk, w_xv, w_xo, norm_ffn, w_up, w_down, norm_final):
    depth = w_in.shape[0]
    bp, _, d = x_prompt.shape
    bs = x_sample.shape[0]
    n_mem = mem_prompt.shape[1]
    d_conv = conv_w.shape[2]
    xd = d // X_HEADS
    hg_lb = hg_lb.astype(F32)
    xp, xs = x_prompt, x_sample
    conv_p, hg_p, mk_l, mv_l, conv_s, hg_s = [], [], [], [], [], []
    for l in range(depth):
        wb = lambda w: w[l].astype(BF16)
        mix_w = (hg_lb, norm_mix[l], wb(w_in), conv_w[l], hg_norm[l], wb(w_conv_out), wb(w_hg_out), wb(w_o))
        ffn_w = (norm_x[l], wb(w_xq), wb(w_xo), norm_ffn[l], wb(w_up), wb(w_down), norm_final)
        last = l == depth - 1
        mk, mv = _mem_kv(mem_prompt, norm_mem[l], wb(w_xk), wb(w_xv))
        xp, cst, st = _mixer(xp, jnp.zeros((bp, CONV_W - 1, d_conv), F32),
                             jnp.zeros((bp, HG_HEADS, HG_DIM, HG_DIM), F32), *mix_w, layer=l)
        xp = _xffn(xp, mk, mv, *ffn_w, final_norm=last)
        conv_p.append(cst)
        hg_p.append(st)
        mk_l.append(mk.reshape(bp, n_mem, X_HEADS, xd))
        mv_l.append(mv.reshape(bp, n_mem, X_HEADS, xd))
        xs, cst, st = _mixer(xs, state_conv[l], state_hgrn[l], *mix_w, layer=l)
        xs = _xffn(xs, cache_mem_k[l].reshape(bs, n_mem, d), cache_mem_v[l].reshape(bs, n_mem, d), *ffn_w,
                   final_norm=last)
        conv_s.append(cst)
        hg_s.append(st)
    return (xp, xs, jnp.stack(conv_p), jnp.stack(hg_p), jnp.stack(mk_l), jnp.stack(mv_l),
            jnp.stack(conv_s), jnp.stack(hg_s))
```

```python
import functools

import jax
import jax.numpy as jnp
from jax import lax
from jax.experimental import pallas as pl
from jax.experimental.pallas import tpu as pltpu

F32 = jnp.float32
BF16 = jnp.bfloat16
EPS = 1e-6

CONV_W = 3
HG_HEADS = 4
HG_DIM = 128
X_HEADS = 4
DECAY_BLOCK = 16
HG_CHUNK = 64
CONV_PAD = 8
FF_TILE = 1024
GATE_TILE = 256

V7X_VMEM_BYTES = 64 * 1024 * 1024
VMEM_LIMIT_BYTES = V7X_VMEM_BYTES - 8 * 1024 * 1024

_NT = (((1,), (1,)), ((), ()))
_TN = (((0,), (0,)), ((), ()))


def _rms(x, g):
    return x * lax.rsqrt(jnp.mean(x * x, axis=-1, keepdims=True) + EPS) * g


def _dot(a, b):
    return jnp.dot(a, b, preferred_element_type=F32)


def _cross_block_pieces(lo, hi):
    if hi - lo <= 1:
        return []
    mid = (lo + hi) // 2
    return ([(i, list(range(lo, mid))) for i in range(mid, hi)]
            + _cross_block_pieces(lo, mid) + _cross_block_pieces(mid, hi))


def _pair_mask(chunk, pieces):
    n_cols = chunk + DECAY_BLOCK * sum(len(js) for _, js in pieces)
    t = lax.broadcasted_iota(jnp.int32, (chunk, n_cols), 0)
    c = lax.broadcasted_iota(jnp.int32, (chunk, n_cols), 1)
    blk_shift = DECAY_BLOCK.bit_length() - 1
    t_blk = t >> blk_shift
    mask = (c < chunk) & ((c >> blk_shift) == t_blk) & (c <= t)
    col = chunk
    for i, js in pieces:
        width = DECAY_BLOCK * len(js)
        mask = mask | ((c >= col) & (c < col + width) & (t_blk == i))
        col += width
    return mask


def _mixer_kernel(x_ref, pre_ref, s0_ref, hglb_ref, nmix_ref, win_ref, cw_ref, hgn_ref, wco_ref, who_ref, wo_ref,
                  x1_ref, cst_ref, sout_ref,
                  st_scr, ubuf, qp_scr, kn_scr, v_scr, a_scr, o_scr, *, layer, tm, chunk):
    d_conv = cw_ref.shape[1]
    hg_w = HG_HEADS * HG_DIM
    d_model = x_ref.shape[1]
    t_idx = pl.program_id(1)
    tail = CONV_W - 1

    @pl.when(t_idx == 0)
    def _load_state():
        for hh in range(HG_HEADS):
            st_scr[hh] = s0_ref[hh].T
        ubuf[CONV_PAD - tail:CONV_PAD, :] = pre_ref[...]

    x = x_ref[...]
    h = _rms(x, nmix_ref[...]).astype(BF16)

    c0 = 3 * d_conv
    g0 = c0 + 3 * hg_w

    ph = _dot(h, win_ref[:, c0:g0])
    pc = _dot(h, win_ref[:, 0:c0])

    hq = ph[:, 0:hg_w]
    hf = ph[:, hg_w:2 * hg_w]
    e_lb = jnp.exp(hglb_ref[...] - jnp.max(hglb_ref[...], axis=0, keepdims=True))
    lb = jnp.sum(e_lb[0:layer + 1], axis=0, keepdims=True) / jnp.sum(e_lb, axis=0, keepdims=True)
    q = hq * jax.nn.sigmoid(hq)
    f = lb + (1.0 - lb) * jax.nn.sigmoid(hf)
    a = jnp.log(f)
    row_in_blk = lax.broadcasted_iota(jnp.int32, (tm, hg_w), 0) & (DECAY_BLOCK - 1)
    shift = 1
    while shift < DECAY_BLOCK:
        a = a + jnp.where(row_in_blk >= shift, pltpu.roll(a, shift, 0), 0.0)
        shift *= 2
    qp_scr[...] = q * jnp.exp(a)
    kn_scr[...] = (1.0 - f) * jnp.exp(-a)
    v_scr[...] = ph[:, 2 * hg_w:3 * hg_w].astype(BF16)
    a_scr[...] = a

    cb = pc[:, 0:d_conv]
    u = pc[:, d_conv:2 * d_conv] * pc[:, 2 * d_conv:3 * d_conv]
    ubuf[CONV_PAD:CONV_PAD + tm, :] = u
    cw = cw_ref[...]
    y_conv = ubuf[CONV_PAD - 2:CONV_PAD - 2 + tm, :] * cw[0:1]
    for j in range(1, CONV_W):
        y_conv = y_conv + ubuf[CONV_PAD - 2 + j:CONV_PAD - 2 + j + tm, :] * cw[j:j + 1]
    u_tail = ubuf[CONV_PAD + tm - tail:CONV_PAD + tm, :]
    ubuf[CONV_PAD - tail:CONV_PAD, :] = u_tail
    cst_ref[...] = u_tail

    gate_cols = hg_w + 2 * d_model
    gate_out = []

    def gate_tile(i):
        lo = i * GATE_TILE
        g = _dot(h, win_ref[:, g0 + lo:g0 + lo + GATE_TILE])
        gate_out.append(g * jax.nn.sigmoid(g) if lo < hg_w else jax.nn.sigmoid(g))

    nblk = chunk // DECAY_BLOCK
    pieces = _cross_block_pieces(0, nblk)
    mask = _pair_mask(chunk, pieces)

    def blk(arr, j):
        return arr[j * DECAY_BLOCK:(j + 1) * DECAY_BLOCK, :]

    def chunk_operands(c):
        r0 = c * chunk
        qp = qp_scr[r0:r0 + chunk, :]
        kn = kn_scr[r0:r0 + chunk, :]
        v = v_scr[r0:r0 + chunk, :]
        cs = [jnp.zeros((1, hg_w), F32)]
        for j in range(nblk):
            row = r0 + (j + 1) * DECAY_BLOCK - 1
            cs.append(cs[j] + a_scr[row:row + 1, :])
        k_parts, v_parts = [kn], [v]
        for i, js in pieces:
            for j in js:
                k_parts.append(blk(kn, j) * jnp.exp(cs[i] - cs[j]))
                v_parts.append(blk(v, j))
        return dict(
            qp=qp.astype(BF16), v=v, dec=jnp.exp(cs[nblk]),
            k_cat=jnp.concatenate(k_parts, axis=0).astype(BF16),
            v_cat=jnp.concatenate(v_parts, axis=0),
            q_in=jnp.concatenate([blk(qp, j) if j == 0 else blk(qp, j) * jnp.exp(cs[j]) for j in range(nblk)],
                                 axis=0).astype(BF16),
            k_st=jnp.concatenate([blk(kn, j) * jnp.exp(cs[nblk] - cs[j]) for j in range(nblk)],
                                 axis=0).astype(BF16))

    n_chunks = tm // chunk
    n_tiles = gate_cols // GATE_TILE
    heads = [slice(hh * HG_DIM, (hh + 1) * HG_DIM) for hh in range(HG_HEADS)]
    early_tiles = min(n_tiles, (hg_w + d_model) // GATE_TILE)
    for i in range(early_tiles):
        gate_tile(i)
    y_a = _dot((cb * y_conv).astype(BF16), wco_ref[...])
    ops = [chunk_operands(c) for c in range(n_chunks)]
    pair = [[lax.dot_general(op["qp"][:, sl], op["k_cat"][:, sl], _NT, preferred_element_type=F32) for sl in heads]
            for op in ops]
    upd = [[lax.dot_general(op["v"][:, sl], op["k_st"][:, sl], _TN, preferred_element_type=F32) for sl in heads]
           for op in ops]
    for i in range(early_tiles, n_tiles):
        gate_tile(i)
    st = [st_scr[hh] for hh in range(HG_HEADS)]
    st_in = []
    for c, op in enumerate(ops):
        st_in.append([s.astype(BF16) for s in st])
        st = [op["dec"][:, sl] * st[hh] + upd[c][hh] for hh, sl in enumerate(heads)]
    for hh in range(HG_HEADS):
        st_scr[hh] = st[hh]
    for c, op in enumerate(ops):
        for hh, sl in enumerate(heads):
            w = jnp.where(mask, pair[c][hh], 0.0).astype(BF16)
            o_scr[c * chunk:(c + 1) * chunk, sl] = (
                _dot(w, op["v_cat"][:, sl])
                + lax.dot_general(op["q_in"][:, sl], st_in[c][hh], _NT, preferred_element_type=F32))

    n_hg = hg_w // GATE_TILE
    n_ga = d_model // GATE_TILE
    o_all = o_scr[...]
    hgn = hgn_ref[...]
    normed = []
    for hh in range(HG_HEADS):
        normed.append(_rms(o_all[:, hh * HG_DIM:(hh + 1) * HG_DIM], hgn))
    o_n = jnp.concatenate(normed, axis=1) * jnp.concatenate(gate_out[:n_hg], axis=1)
    y_b = _dot(o_n.astype(BF16), who_ref[...])
    merged = (jnp.concatenate(gate_out[n_hg:n_hg + n_ga], axis=1) * y_a
              + jnp.concatenate(gate_out[n_hg + n_ga:], axis=1) * y_b)
    x1_ref[...] = x + _dot(merged.astype(BF16), wo_ref[...])

    @pl.when(t_idx == pl.num_programs(1) - 1)
    def _store_state():
        for hh in range(HG_HEADS):
            sout_ref[hh] = st_scr[hh].T


def _xffn_kernel(x_ref, mk_ref, mv_ref, nx_ref, wq_ref, wxo_ref, nf_ref, wup_ref, wdn_ref, nfin_ref,
                 y_ref, mk_scr, mv_scr, *, final_norm):
    d_model = x_ref.shape[1]
    d_ff = wup_ref.shape[1]
    xd = d_model // X_HEADS

    @pl.when(pl.program_id(1) == 0)
    def _load_memory():
        mk_scr[...] = mk_ref[...].astype(BF16)
        mv_scr[...] = mv_ref[...].astype(BF16)

    x = x_ref[...]
    h = _rms(x, nx_ref[...]).astype(BF16)
    q = _dot(h, wq_ref[...])
    heads = [slice(hh * xd, (hh + 1) * xd) for hh in range(X_HEADS)]
    q_b = q.astype(BF16)
    scores = [lax.dot_general(q_b[:, sl], mk_scr[:, sl], _NT, preferred_element_type=F32) * (xd ** -0.5)
              for sl in heads]
    probs = []
    for s in scores:
        e = jnp.exp(s - jnp.max(s, axis=-1, keepdims=True))
        probs.append((e / jnp.sum(e, axis=-1, keepdims=True)).astype(BF16))
    o = jnp.concatenate([_dot(p, mv_scr[:, sl]) for p, sl in zip(probs, heads)], axis=1)
    x2 = x + _dot(o.astype(BF16), wxo_ref[...])

    h3 = _rms(x2, nf_ref[...]).astype(BF16)
    acc = None
    for j in range(d_ff // FF_TILE):
        act = jnp.maximum(_dot(h3, wup_ref[:, j * FF_TILE:(j + 1) * FF_TILE]), 0.0)
        part = _dot((act * act).astype(BF16), wdn_ref[j * FF_TILE:(j + 1) * FF_TILE, :])
        acc = part if acc is None else acc + part
    x3 = x2 + acc
    y_ref[...] = _rms(x3, nfin_ref[...]) if final_norm else x3


def _memkv_kernel(mem_ref, g_ref, wk_ref, wv_ref, mk_ref, mv_ref):
    mh = _rms(mem_ref[...], g_ref[...]).astype(BF16)
    mk_ref[...] = _dot(mh, wk_ref[...])
    mv_ref[...] = _dot(mh, wv_ref[...])


def _resident(shape):
    return pl.BlockSpec(shape, lambda *_: (0,) * len(shape), pipeline_mode=pl.Buffered(1))


def _token_tile(t_len):
    for tm in (512, 256, 128, 64, 32, 16):
        if t_len % tm == 0:
            return tm
    raise ValueError(f"sequence length {t_len} must be a multiple of {DECAY_BLOCK}")


def _params():
    return pltpu.CompilerParams(dimension_semantics=("arbitrary", "arbitrary"), vmem_limit_bytes=VMEM_LIMIT_BYTES)


def _mixer(x, prefix, s0, hg_lb, norm_mix, w_in, conv_w, hg_norm, w_conv_out, w_hg_out, w_o, *, layer):
    b, t_len, d = x.shape
    d_conv = conv_w.shape[1]
    hg_w = HG_HEADS * HG_DIM
    tm = _token_tile(t_len)
    chunk = min(HG_CHUNK, tm)
    tok = pl.BlockSpec((None, tm, d), lambda i, j: (i, j, 0))
    per_batch = lambda *shape: pl.BlockSpec((None,) + shape, lambda i, j: (i,) + (0,) * len(shape))
    return pl.pallas_call(
        functools.partial(_mixer_kernel, layer=layer, tm=tm, chunk=chunk),
        grid=(b, t_len // tm),
        in_specs=[tok, per_batch(CONV_W - 1, d_conv), per_batch(HG_HEADS, HG_DIM, HG_DIM),
                  _resident(hg_lb.shape), _resident((1, d)), _resident(w_in.shape), _resident(conv_w.shape),
                  _resident((1, HG_DIM)), _resident(w_conv_out.shape), _resident(w_hg_out.shape),
                  _resident(w_o.shape)],
        out_specs=[tok, per_batch(CONV_W - 1, d_conv), per_batch(HG_HEADS, HG_DIM, HG_DIM)],
        out_shape=[jax.ShapeDtypeStruct((b, t_len, d), F32),
                   jax.ShapeDtypeStruct((b, CONV_W - 1, d_conv), F32),
                   jax.ShapeDtypeStruct((b, HG_HEADS, HG_DIM, HG_DIM), F32)],
        scratch_shapes=[pltpu.VMEM((HG_HEADS, HG_DIM, HG_DIM), F32),
                        pltpu.VMEM((CONV_PAD + tm, d_conv), F32),
                        pltpu.VMEM((tm, hg_w), F32), pltpu.VMEM((tm, hg_w), F32),
                        pltpu.VMEM((tm, hg_w), BF16), pltpu.VMEM((tm, hg_w), F32),
                        pltpu.VMEM((tm, hg_w), F32)],
        compiler_params=_params(),
        name="mixer",
    )(x, prefix, s0, hg_lb, norm_mix.reshape(1, d), w_in, conv_w, hg_norm.reshape(1, HG_DIM),
      w_conv_out, w_hg_out, w_o)


def _xffn(x, mk, mv, norm_x, w_xq, w_xo, norm_ffn, w_up, w_down, norm_final, *, final_norm):
    b, t_len, d = x.shape
    n_mem = mk.shape[1]
    tm = _token_tile(t_len)
    tok = pl.BlockSpec((None, tm, d), lambda i, j: (i, j, 0))
    mem = pl.BlockSpec((None, n_mem, d), lambda i, j: (i, 0, 0))
    return pl.pallas_call(
        functools.partial(_xffn_kernel, final_norm=final_norm),
        grid=(b, t_len // tm),
        in_specs=[tok, mem, mem, _resident((1, d)), _resident(w_xq.shape), _resident(w_xo.shape),
                  _resident((1, d)), _resident(w_up.shape), _resident(w_down.shape), _resident((1, d))],
        out_specs=tok,
        out_shape=jax.ShapeDtypeStruct((b, t_len, d), F32),
        scratch_shapes=[pltpu.VMEM((n_mem, d), BF16), pltpu.VMEM((n_mem, d), BF16)],
        compiler_params=_params(),
        name="xffn",
    )(x, mk, mv, norm_x.reshape(1, d), w_xq, w_xo, norm_ffn.reshape(1, d), w_up, w_down, norm_final.reshape(1, d))


def _mem_kv(mem, g, w_xk, w_xv):
    b, n_mem, d = mem.shape
    blk = pl.BlockSpec((None, n_mem, d), lambda i: (i, 0, 0))
    return pl.pallas_call(
        _memkv_kernel,
        grid=(b,),
        in_specs=[blk, _resident((1, d)), _resident(w_xk.shape), _resident(w_xv.shape)],
        out_specs=[blk, blk],
        out_shape=[jax.ShapeDtypeStruct((b, n_mem, d), F32)] * 2,
        compiler_params=pltpu.CompilerParams(dimension_semantics=("arbitrary",), vmem_limit_bytes=VMEM_LIMIT_BYTES),
        name="mem_kv",
    )(mem, g.reshape(1, d), w_xk, w_xv)


def kernel(x_prompt, x_sample, mem_prompt, state_conv, state_hgrn, cache_mem_k, cache_mem_v, norm_mix, w_in, conv_w, hg_lb, hg_norm, w_conv_out, w_hg_out, w_o, norm_x, norm_mem, w_xq, w_xk, w_xv, w_xo, norm_ffn, w_up, w_down, norm_final):
    depth = w_in.shape[0]
    bp, _, d = x_prompt.shape
    bs = x_sample.shape[0]
    n_mem = mem_prompt.shape[1]
    d_conv = conv_w.shape[2]
    xd = d // X_HEADS
    hg_lb = hg_lb.astype(F32)
    xp, xs = x_prompt, x_sample
    conv_p, hg_p, mk_l, mv_l, conv_s, hg_s = [], [], [], [], [], []
    for l in range(depth):
        wb = lambda w: w[l].astype(BF16)
        mix_w = (hg_lb, norm_mix[l], wb(w_in), conv_w[l], hg_norm[l], wb(w_conv_out), wb(w_hg_out), wb(w_o))
        ffn_w = (norm_x[l], wb(w_xq), wb(w_xo), norm_ffn[l], wb(w_up), wb(w_down), norm_final)
        last = l == depth - 1
        mk, mv = _mem_kv(mem_prompt, norm_mem[l], wb(w_xk), wb(w_xv))
        xp, cst, st = _mixer(xp, jnp.zeros((bp, CONV_W - 1, d_conv), F32),
                             jnp.zeros((bp, HG_HEADS, HG_DIM, HG_DIM), F32), *mix_w, layer=l)
        xp = _xffn(xp, mk, mv, *ffn_w, final_norm=last)
        conv_p.append(cst)
        hg_p.append(st)
        mk_l.append(mk.reshape(bp, n_mem, X_HEADS, xd))
        mv_l.append(mv.reshape(bp, n_mem, X_HEADS, xd))
        xs, cst, st = _mixer(xs, state_conv[l], state_hgrn[l], *mix_w, layer=l)
        xs = _xffn(xs, cache_mem_k[l].reshape(bs, n_mem, d), cache_mem_v[l].reshape(bs, n_mem, d), *ffn_w,
                   final_norm=last)
        conv_s.append(cst)
        hg_s.append(st)
    return (xp, xs, jnp.stack(conv_p), jnp.stack(hg_p), jnp.stack(mk_l), jnp.stack(mv_l),
            jnp.stack(conv_s), jnp.stack(hg_s))
```

```python
import functools

import jax
import jax.numpy as jnp
from jax import lax
from jax.experimental import pallas as pl
from jax.experimental.pallas import tpu as pltpu

F32 = jnp.float32
BF16 = jnp.bfloat16
EPS = 1e-6

CONV_W = 3
HG_HEADS = 4
HG_DIM = 128
X_HEADS = 4
DECAY_BLOCK = 16
HG_CHUNK = 64
CONV_PAD = 8
FF_TILE = 1024
GATE_TILE = 256
MIXER_SEQS_PER_TILE = 16
XFFN_SEQS_PER_TILE = 4

V7X_VMEM_BYTES = 64 * 1024 * 1024
VMEM_LIMIT_BYTES = V7X_VMEM_BYTES - 8 * 1024 * 1024

_NT = (((1,), (1,)), ((), ()))
_TN = (((0,), (0,)), ((), ()))


def _rms(x, g):
    return x * lax.rsqrt(jnp.mean(x * x, axis=-1, keepdims=True) + EPS) * g


def _dot(a, b):
    return jnp.dot(a, b, preferred_element_type=F32)


def _cross_block_pieces(lo, hi):
    if hi - lo <= 1:
        return []
    mid = (lo + hi) // 2
    return ([(i, list(range(lo, mid))) for i in range(mid, hi)]
            + _cross_block_pieces(lo, mid) + _cross_block_pieces(mid, hi))


def _pair_mask(chunk, pieces):
    n_cols = chunk + DECAY_BLOCK * sum(len(js) for _, js in pieces)
    t = lax.broadcasted_iota(jnp.int32, (chunk, n_cols), 0)
    c = lax.broadcasted_iota(jnp.int32, (chunk, n_cols), 1)
    blk_shift = DECAY_BLOCK.bit_length() - 1
    t_blk = t >> blk_shift
    mask = (c < chunk) & ((c >> blk_shift) == t_blk) & (c <= t)
    col = chunk
    for i, js in pieces:
        width = DECAY_BLOCK * len(js)
        mask = mask | ((c >= col) & (c < col + width) & (t_blk == i))
        col += width
    return mask


def _mixer_kernel(x_ref, pre_ref, s0_ref, hglb_ref, nmix_ref, win_ref, cw_ref, hgn_ref, wco_ref, who_ref, wo_ref,
                  x1_ref, cst_ref, sout_ref,
                  st_scr, ubuf, qp_scr, kn_scr, v_scr, a_scr, o_scr, *, layer, seqs, tq, chunk):
    d_conv = cw_ref.shape[1]
    hg_w = HG_HEADS * HG_DIM
    d_model = x_ref.shape[2]
    tm = seqs * tq
    t_idx = pl.program_id(1)
    tail = CONV_W - 1
    seq_pitch = CONV_PAD + tq

    @pl.when(t_idx == 0)
    def _load_state():
        for s in range(seqs):
            for hh in range(HG_HEADS):
                st_scr[s * HG_HEADS + hh] = s0_ref[s, hh].T
            ubuf[s * seq_pitch + CONV_PAD - tail:s * seq_pitch + CONV_PAD, :] = pre_ref[s]

    x = x_ref[...].reshape(tm, d_model)
    h = _rms(x, nmix_ref[...]).astype(BF16)

    c0 = 3 * d_conv
    g0 = c0 + 3 * hg_w

    ph = _dot(h, win_ref[:, c0:g0])
    pc = _dot(h, win_ref[:, 0:c0])

    hq = ph[:, 0:hg_w]
    hf = ph[:, hg_w:2 * hg_w]
    e_lb = jnp.exp(hglb_ref[...] - jnp.max(hglb_ref[...], axis=0, keepdims=True))
    lb = jnp.sum(e_lb[0:layer + 1], axis=0, keepdims=True) / jnp.sum(e_lb, axis=0, keepdims=True)
    q = hq * jax.nn.sigmoid(hq)
    f = lb + (1.0 - lb) * jax.nn.sigmoid(hf)
    a = jnp.log(f)
    row_in_blk = lax.broadcasted_iota(jnp.int32, (tm, hg_w), 0) & (DECAY_BLOCK - 1)
    shift = 1
    while shift < DECAY_BLOCK:
        a = a + jnp.where(row_in_blk >= shift, pltpu.roll(a, shift, 0), 0.0)
        shift *= 2
    qp_scr[...] = q * jnp.exp(a)
    kn_scr[...] = (1.0 - f) * jnp.exp(-a)
    v_scr[...] = ph[:, 2 * hg_w:3 * hg_w].astype(BF16)
    a_scr[...] = a

    cb = pc[:, 0:d_conv]
    u = pc[:, d_conv:2 * d_conv] * pc[:, 2 * d_conv:3 * d_conv]
    cw = cw_ref[...]
    y_parts = []
    for s in range(seqs):
        base = s * seq_pitch + CONV_PAD
        ubuf[base:base + tq, :] = u[s * tq:(s + 1) * tq, :]
        y_s = ubuf[base - tail:base - tail + tq, :] * cw[0:1]
        for j in range(1, CONV_W):
            y_s = y_s + ubuf[base - tail + j:base - tail + j + tq, :] * cw[j:j + 1]
        y_parts.append(y_s)
        u_tail = ubuf[base + tq - tail:base + tq, :]
        ubuf[base - tail:base, :] = u_tail
        cst_ref[s] = u_tail
    y_conv = y_parts[0] if seqs == 1 else jnp.concatenate(y_parts, axis=0)

    gate_cols = hg_w + 2 * d_model
    gate_out = []

    def gate_tile(i):
        lo = i * GATE_TILE
        g = _dot(h, win_ref[:, g0 + lo:g0 + lo + GATE_TILE])
        gate_out.append(g * jax.nn.sigmoid(g) if lo < hg_w else jax.nn.sigmoid(g))

    nblk = chunk // DECAY_BLOCK
    pieces = _cross_block_pieces(0, nblk)
    mask = _pair_mask(chunk, pieces)

    def blk(arr, j):
        return arr[j * DECAY_BLOCK:(j + 1) * DECAY_BLOCK, :]

    def chunk_operands(c):
        r0 = c * chunk
        qp = qp_scr[r0:r0 + chunk, :]
        kn = kn_scr[r0:r0 + chunk, :]
        v = v_scr[r0:r0 + chunk, :]
        cs = [jnp.zeros((1, hg_w), F32)]
        for j in range(nblk):
            row = r0 + (j + 1) * DECAY_BLOCK - 1
            cs.append(cs[j] + a_scr[row:row + 1, :])
        k_parts, v_parts = [kn], [v]
        for i, js in pieces:
            for j in js:
                k_parts.append(blk(kn, j) * jnp.exp(cs[i] - cs[j]))
                v_parts.append(blk(v, j))
        return dict(
            qp=qp.astype(BF16), v=v, dec=jnp.exp(cs[nblk]),
            k_cat=jnp.concatenate(k_parts, axis=0).astype(BF16),
            v_cat=jnp.concatenate(v_parts, axis=0),
            q_in=jnp.concatenate([blk(qp, j) if j == 0 else blk(qp, j) * jnp.exp(cs[j]) for j in range(nblk)],
                                 axis=0).astype(BF16),
            k_st=jnp.concatenate([blk(kn, j) * jnp.exp(cs[nblk] - cs[j]) for j in range(nblk)],
                                 axis=0).astype(BF16))

    n_chunks = tm // chunk
    n_tiles = gate_cols // GATE_TILE
    heads = [slice(hh * HG_DIM, (hh + 1) * HG_DIM) for hh in range(HG_HEADS)]
    early_tiles = min(n_tiles, (hg_w + d_model) // GATE_TILE)
    for i in range(early_tiles):
        gate_tile(i)
    y_a = _dot((cb * y_conv).astype(BF16), wco_ref[...])
    ops = [chunk_operands(c) for c in range(n_chunks)]
    pair = [[lax.dot_general(op["qp"][:, sl], op["k_cat"][:, sl], _NT, preferred_element_type=F32) for sl in heads]
            for op in ops]
    upd = [[lax.dot_general(op["v"][:, sl], op["k_st"][:, sl], _TN, preferred_element_type=F32) for sl in heads]
           for op in ops]
    for i in range(early_tiles, n_tiles):
        gate_tile(i)
    chunks_per_seq = tq // chunk
    st_in = []
    for s in range(seqs):
        st = [st_scr[s * HG_HEADS + hh] for hh in range(HG_HEADS)]
        for c in range(s * chunks_per_seq, (s + 1) * chunks_per_seq):
            st_in.append([m.astype(BF16) for m in st])
            st = [ops[c]["dec"][:, sl] * st[hh] + upd[c][hh] for hh, sl in enumerate(heads)]
        for hh in range(HG_HEADS):
            st_scr[s * HG_HEADS + hh] = st[hh]
    for c, op in enumerate(ops):
        for hh, sl in enumerate(heads):
            w = jnp.where(mask, pair[c][hh], 0.0).astype(BF16)
            o_scr[c * chunk:(c + 1) * chunk, sl] = (
                _dot(w, op["v_cat"][:, sl])
                + lax.dot_general(op["q_in"][:, sl], st_in[c][hh], _NT, preferred_element_type=F32))

    n_hg = hg_w // GATE_TILE
    n_ga = d_model // GATE_TILE
    o_all = o_scr[...]
    hgn = hgn_ref[...]
    normed = []
    for hh in range(HG_HEADS):
        normed.append(_rms(o_all[:, hh * HG_DIM:(hh + 1) * HG_DIM], hgn))
    o_n = jnp.concatenate(normed, axis=1) * jnp.concatenate(gate_out[:n_hg], axis=1)
    y_b = _dot(o_n.astype(BF16), who_ref[...])
    merged = (jnp.concatenate(gate_out[n_hg:n_hg + n_ga], axis=1) * y_a
              + jnp.concatenate(gate_out[n_hg + n_ga:], axis=1) * y_b)
    x1_ref[...] = (x + _dot(merged.astype(BF16), wo_ref[...])).reshape(seqs, tq, d_model)

    @pl.when(t_idx == pl.num_programs(1) - 1)
    def _store_state():
        for s in range(seqs):
            for hh in range(HG_HEADS):
                sout_ref[s, hh] = st_scr[s * HG_HEADS + hh].T


def _xffn_kernel(x_ref, mk_ref, mv_ref, nx_ref, wq_ref, wxo_ref, nf_ref, wup_ref, wdn_ref, nfin_ref,
                 y_ref, mk_scr, mv_scr, *, final_norm):
    seqs, tq, d_model = x_ref.shape
    d_ff = wup_ref.shape[1]
    xd = d_model // X_HEADS

    @pl.when(pl.program_id(1) == 0)
    def _load_memory():
        mk_scr[...] = mk_ref[...].astype(BF16)
        mv_scr[...] = mv_ref[...].astype(BF16)

    x = x_ref[...].reshape(seqs * tq, d_model)
    h = _rms(x, nx_ref[...]).astype(BF16)
    q = _dot(h, wq_ref[...])
    heads = [slice(hh * xd, (hh + 1) * xd) for hh in range(X_HEADS)]
    q_b = q.astype(BF16)
    scores = [[lax.dot_general(q_b[s * tq:(s + 1) * tq, sl], mk_scr[s, :, sl], _NT, preferred_element_type=F32)
               * (xd ** -0.5) for sl in heads] for s in range(seqs)]
    probs = []
    for per_seq in scores:
        probs.append([])
        for sc in per_seq:
            e = jnp.exp(sc - jnp.max(sc, axis=-1, keepdims=True))
            probs[-1].append((e / jnp.sum(e, axis=-1, keepdims=True)).astype(BF16))
    o_rows = [jnp.concatenate([_dot(p, mv_scr[s, :, sl]) for p, sl in zip(probs[s], heads)], axis=1)
              for s in range(seqs)]
    o = o_rows[0] if seqs == 1 else jnp.concatenate(o_rows, axis=0)
    x2 = x + _dot(o.astype(BF16), wxo_ref[...])

    h3 = _rms(x2, nf_ref[...]).astype(BF16)
    acc = None
    for j in range(d_ff // FF_TILE):
        act = jnp.maximum(_dot(h3, wup_ref[:, j * FF_TILE:(j + 1) * FF_TILE]), 0.0)
        part = _dot((act * act).astype(BF16), wdn_ref[j * FF_TILE:(j + 1) * FF_TILE, :])
        acc = part if acc is None else acc + part
    x3 = x2 + acc
    y = _rms(x3, nfin_ref[...]) if final_norm else x3
    y_ref[...] = y.reshape(seqs, tq, d_model)


def _memkv_kernel(mem_ref, g_ref, wk_ref, wv_ref, mk_ref, mv_ref):
    mh = _rms(mem_ref[...], g_ref[...]).astype(BF16)
    mk_ref[...] = _dot(mh, wk_ref[...])
    mv_ref[...] = _dot(mh, wv_ref[...])


def _resident(shape):
    return pl.BlockSpec(shape, lambda *_: (0,) * len(shape), pipeline_mode=pl.Buffered(1))


def _token_tile(t_len, largest=512):
    for tm in (1024, 512, 256, 128, 64, 32, 16):
        if tm <= largest and t_len % tm == 0:
            return tm
    raise ValueError(f"sequence length {t_len} must be a multiple of {DECAY_BLOCK}")


def _params():
    return pltpu.CompilerParams(dimension_semantics=("arbitrary", "arbitrary"), vmem_limit_bytes=VMEM_LIMIT_BYTES)


def _tiling(b, t_len, largest, max_seqs):
    if t_len > HG_CHUNK:
        return 1, _token_tile(t_len, largest)
    seqs = max(s for s in range(1, b + 1) if b % s == 0 and s <= max_seqs and s * t_len <= largest)
    return seqs, t_len


def _mixer(x, prefix, s0, hg_lb, norm_mix, w_in, conv_w, hg_norm, w_conv_out, w_hg_out, w_o, *, layer):
    b, t_len, d = x.shape
    d_conv = conv_w.shape[1]
    hg_w = HG_HEADS * HG_DIM
    seqs, tq = _tiling(b, t_len, largest=512, max_seqs=MIXER_SEQS_PER_TILE)
    tm = seqs * tq
    chunk = min(HG_CHUNK, tq)
    tok = pl.BlockSpec((seqs, tq, d), lambda i, j: (i, j, 0))
    per_batch = lambda *shape: pl.BlockSpec((seqs,) + shape, lambda i, j: (i,) + (0,) * len(shape))
    return pl.pallas_call(
        functools.partial(_mixer_kernel, layer=layer, seqs=seqs, tq=tq, chunk=chunk),
        grid=(b // seqs, t_len // tq),
        in_specs=[tok, per_batch(CONV_W - 1, d_conv), per_batch(HG_HEADS, HG_DIM, HG_DIM),
                  _resident(hg_lb.shape), _resident((1, d)), _resident(w_in.shape), _resident(conv_w.shape),
                  _resident((1, HG_DIM)), _resident(w_conv_out.shape), _resident(w_hg_out.shape),
                  _resident(w_o.shape)],
        out_specs=[tok, per_batch(CONV_W - 1, d_conv), per_batch(HG_HEADS, HG_DIM, HG_DIM)],
        out_shape=[jax.ShapeDtypeStruct((b, t_len, d), F32),
                   jax.ShapeDtypeStruct((b, CONV_W - 1, d_conv), F32),
                   jax.ShapeDtypeStruct((b, HG_HEADS, HG_DIM, HG_DIM), F32)],
        scratch_shapes=[pltpu.VMEM((seqs * HG_HEADS, HG_DIM, HG_DIM), F32),
                        pltpu.VMEM((seqs * (CONV_PAD + tq), d_conv), F32),
                        pltpu.VMEM((tm, hg_w), F32), pltpu.VMEM((tm, hg_w), F32),
                        pltpu.VMEM((tm, hg_w), BF16), pltpu.VMEM((tm, hg_w), F32),
                        pltpu.VMEM((tm, hg_w), F32)],
        compiler_params=_params(),
        name="mixer",
    )(x, prefix, s0, hg_lb, norm_mix.reshape(1, d), w_in, conv_w, hg_norm.reshape(1, HG_DIM),
      w_conv_out, w_hg_out, w_o)


def _xffn(x, mk, mv, norm_x, w_xq, w_xo, norm_ffn, w_up, w_down, norm_final, *, final_norm):
    b, t_len, d = x.shape
    n_mem = mk.shape[1]
    seqs, tq = _tiling(b, t_len, largest=1024, max_seqs=XFFN_SEQS_PER_TILE)
    tok = pl.BlockSpec((seqs, tq, d), lambda i, j: (i, j, 0))
    mem = pl.BlockSpec((seqs, n_mem, d), lambda i, j: (i, 0, 0))
    return pl.pallas_call(
        functools.partial(_xffn_kernel, final_norm=final_norm),
        grid=(b // seqs, t_len // tq),
        in_specs=[tok, mem, mem, _resident((1, d)), _resident(w_xq.shape), _resident(w_xo.shape),
                  _resident((1, d)), _resident(w_up.shape), _resident(w_down.shape), _resident((1, d))],
        out_specs=tok,
        out_shape=jax.ShapeDtypeStruct((b, t_len, d), F32),
        scratch_shapes=[pltpu.VMEM((seqs, n_mem, d), BF16), pltpu.VMEM((seqs, n_mem, d), BF16)],
        compiler_params=_params(),
        name="xffn",
    )(x, mk, mv, norm_x.reshape(1, d), w_xq, w_xo, norm_ffn.reshape(1, d), w_up, w_down, norm_final.reshape(1, d))


def _mem_kv(mem, g, w_xk, w_xv):
    b, n_mem, d = mem.shape
    blk = pl.BlockSpec((None, n_mem, d), lambda i: (i, 0, 0))
    return pl.pallas_call(
        _memkv_kernel,
        grid=(b,),
        in_specs=[blk, _resident((1, d)), _resident(w_xk.shape), _resident(w_xv.shape)],
        out_specs=[blk, blk],
        out_shape=[jax.ShapeDtypeStruct((b, n_mem, d), F32)] * 2,
        compiler_params=pltpu.CompilerParams(dimension_semantics=("arbitrary",), vmem_limit_bytes=VMEM_LIMIT_BYTES),
        name="mem_kv",
    )(mem, g.reshape(1, d), w_xk, w_xv)


def kernel(x_prompt, x_sample, mem_prompt, state_conv, state_hgrn, cache_mem_k, cache_mem_v, norm_mix, w_in, conv_w, hg_lb, hg_norm, w_conv_out, w_hg_out, w_o, norm_x, norm_mem, w_xq, w_xk, w_xv, w_xo, norm_ffn, w_up, w_down, norm_final):
    depth = w_in.shape[0]
    bp, _, d = x_prompt.shape
    bs = x_sample.shape[0]
    n_mem = mem_prompt.shape[1]
    d_conv = conv_w.shape[2]
    xd = d // X_HEADS
    hg_lb = hg_lb.astype(F32)
    xp, xs = x_prompt, x_sample
    conv_p, hg_p, mk_l, mv_l, conv_s, hg_s = [], [], [], [], [], []
    for l in range(depth):
        wb = lambda w: w[l].astype(BF16)
        mix_w = (hg_lb, norm_mix[l], wb(w_in), conv_w[l], hg_norm[l], wb(w_conv_out), wb(w_hg_out), wb(w_o))
        ffn_w = (norm_x[l], wb(w_xq), wb(w_xo), norm_ffn[l], wb(w_up), wb(w_down), norm_final)
        last = l == depth - 1
        mk, mv = _mem_kv(mem_prompt, norm_mem[l], wb(w_xk), wb(w_xv))
        xp, cst, st = _mixer(xp, jnp.zeros((bp, CONV_W - 1, d_conv), F32),
                             jnp.zeros((bp, HG_HEADS, HG_DIM, HG_DIM), F32), *mix_w, layer=l)
        xp = _xffn(xp, mk, mv, *ffn_w, final_norm=last)
        conv_p.append(cst)
        hg_p.append(st)
        mk_l.append(mk.reshape(bp, n_mem, X_HEADS, xd))
        mv_l.append(mv.reshape(bp, n_mem, X_HEADS, xd))
        xs, cst, st = _mixer(xs, state_conv[l], state_hgrn[l], *mix_w, layer=l)
        xs = _xffn(xs, cache_mem_k[l].reshape(bs, n_mem, d), cache_mem_v[l].reshape(bs, n_mem, d), *ffn_w,
                   final_norm=last)
        conv_s.append(cst)
        hg_s.append(st)
    return (xp, xs, jnp.stack(conv_p), jnp.stack(hg_p), jnp.stack(mk_l), jnp.stack(mv_l),
            jnp.stack(conv_s), jnp.stack(hg_s))
```

```python
import functools

import jax
import jax.numpy as jnp
from jax import lax
from jax.experimental import pallas as pl
from jax.experimental.pallas import tpu as pltpu

F32 = jnp.float32
BF16 = jnp.bfloat16
EPS = 1e-6

CONV_W = 3
HG_HEADS = 4
HG_DIM = 128
X_HEADS = 4
DECAY_BLOCK = 16
HG_CHUNK = 64
CONV_PAD = 8
FF_TILE = 1024
GATE_TILE = 256
MIXER_SEQS_PER_TILE = 16
XFFN_SEQS_PER_TILE = 4

V7X_VMEM_BYTES = 64 * 1024 * 1024
VMEM_LIMIT_BYTES = V7X_VMEM_BYTES - 8 * 1024 * 1024

_NT = (((1,), (1,)), ((), ()))
_TN = (((0,), (0,)), ((), ()))


def _rms(x, g):
    return x * lax.rsqrt(jnp.mean(x * x, axis=-1, keepdims=True) + EPS) * g


def _dot(a, b):
    return jnp.dot(a, b, preferred_element_type=F32)


def _cross_block_pieces(lo, hi):
    if hi - lo <= 1:
        return []
    mid = (lo + hi) // 2
    return ([(i, list(range(lo, mid))) for i in range(mid, hi)]
            + _cross_block_pieces(lo, mid) + _cross_block_pieces(mid, hi))


def _pair_mask(chunk, pieces):
    n_cols = chunk + DECAY_BLOCK * sum(len(js) for _, js in pieces)
    t = lax.broadcasted_iota(jnp.int32, (chunk, n_cols), 0)
    c = lax.broadcasted_iota(jnp.int32, (chunk, n_cols), 1)
    blk_shift = DECAY_BLOCK.bit_length() - 1
    t_blk = t >> blk_shift
    mask = (c < chunk) & ((c >> blk_shift) == t_blk) & (c <= t)
    col = chunk
    for i, js in pieces:
        width = DECAY_BLOCK * len(js)
        mask = mask | ((c >= col) & (c < col + width) & (t_blk == i))
        col += width
    return mask


def _mixer_kernel(x_ref, pre_ref, s0_ref, hglb_ref, nmix_ref, win_ref, cw_ref, hgn_ref, wco_ref, who_ref, wo_ref,
                  x1_ref, cst_ref, sout_ref,
                  st_scr, ubuf, qp_scr, kn_scr, v_scr, a_scr, o_scr, *, layer, seqs, tq, chunk):
    d_conv = cw_ref.shape[1]
    hg_w = HG_HEADS * HG_DIM
    d_model = x_ref.shape[2]
    tm = seqs * tq
    t_idx = pl.program_id(1)
    tail = CONV_W - 1
    seq_pitch = CONV_PAD + tq

    @pl.when(t_idx == 0)
    def _load_state():
        for s in range(seqs):
            for hh in range(HG_HEADS):
                st_scr[s * HG_HEADS + hh] = s0_ref[s, hh].T
            ubuf[s * seq_pitch + CONV_PAD - tail:s * seq_pitch + CONV_PAD, :] = pre_ref[s]

    x = x_ref[...].reshape(tm, d_model)
    h = _rms(x, nmix_ref[...]).astype(BF16)

    c0 = 3 * d_conv
    g0 = c0 + 3 * hg_w

    ph = _dot(h, win_ref[:, c0:g0])
    pc = _dot(h, win_ref[:, 0:c0])

    hq = ph[:, 0:hg_w]
    hf = ph[:, hg_w:2 * hg_w]
    e_lb = jnp.exp(hglb_ref[...] - jnp.max(hglb_ref[...], axis=0, keepdims=True))
    lb = jnp.sum(e_lb[0:layer + 1], axis=0, keepdims=True) / jnp.sum(e_lb, axis=0, keepdims=True)
    q = hq * jax.nn.sigmoid(hq)
    f = lb + (1.0 - lb) * jax.nn.sigmoid(hf)
    a = jnp.log(f)
    row_in_blk = lax.broadcasted_iota(jnp.int32, (tm, hg_w), 0) & (DECAY_BLOCK - 1)
    shift = 1
    while shift < DECAY_BLOCK:
        a = a + jnp.where(row_in_blk >= shift, pltpu.roll(a, shift, 0), 0.0)
        shift *= 2
    qp_scr[...] = q * jnp.exp(a)
    kn_scr[...] = (1.0 - f) * jnp.exp(-a)
    v_scr[...] = ph[:, 2 * hg_w:3 * hg_w].astype(BF16)
    a_scr[...] = a

    cb = pc[:, 0:d_conv]
    u = pc[:, d_conv:2 * d_conv] * pc[:, 2 * d_conv:3 * d_conv]
    cw = cw_ref[...]
    y_parts = []
    for s in range(seqs):
        base = s * seq_pitch + CONV_PAD
        ubuf[base:base + tq, :] = u[s * tq:(s + 1) * tq, :]
        y_s = ubuf[base - tail:base - tail + tq, :] * cw[0:1]
        for j in range(1, CONV_W):
            y_s = y_s + ubuf[base - tail + j:base - tail + j + tq, :] * cw[j:j + 1]
        y_parts.append(y_s)
        u_tail = ubuf[base + tq - tail:base + tq, :]
        ubuf[base - tail:base, :] = u_tail
        cst_ref[s] = u_tail
    y_conv = y_parts[0] if seqs == 1 else jnp.concatenate(y_parts, axis=0)

    gate_cols = hg_w + 2 * d_model
    gate_out = []

    def gate_tile(i):
        lo = i * GATE_TILE
        g = _dot(h, win_ref[:, g0 + lo:g0 + lo + GATE_TILE])
        gate_out.append(g * jax.nn.sigmoid(g) if lo < hg_w else jax.nn.sigmoid(g))

    nblk = chunk // DECAY_BLOCK
    pieces = _cross_block_pieces(0, nblk)
    mask = _pair_mask(chunk, pieces)

    def blk(arr, j):
        return arr[j * DECAY_BLOCK:(j + 1) * DECAY_BLOCK, :]

    def chunk_operands(c):
        r0 = c * chunk
        qp = qp_scr[r0:r0 + chunk, :]
        kn = kn_scr[r0:r0 + chunk, :]
        v = v_scr[r0:r0 + chunk, :]
        cs = [jnp.zeros((1, hg_w), F32)]
        for j in range(nblk):
            row = r0 + (j + 1) * DECAY_BLOCK - 1
            cs.append(cs[j] + a_scr[row:row + 1, :])
        k_parts, v_parts = [kn], [v]
        for i, js in pieces:
            for j in js:
                k_parts.append(blk(kn, j) * jnp.exp(cs[i] - cs[j]))
                v_parts.append(blk(v, j))
        return dict(
            qp=qp.astype(BF16), v=v, dec=jnp.exp(cs[nblk]),
            k_cat=jnp.concatenate(k_parts, axis=0).astype(BF16),
            v_cat=jnp.concatenate(v_parts, axis=0),
            q_in=jnp.concatenate([blk(qp, j) if j == 0 else blk(qp, j) * jnp.exp(cs[j]) for j in range(nblk)],
                                 axis=0).astype(BF16),
            k_st=jnp.concatenate([blk(kn, j) * jnp.exp(cs[nblk] - cs[j]) for j in range(nblk)],
                                 axis=0).astype(BF16))

    n_chunks = tm // chunk
    n_tiles = gate_cols // GATE_TILE
    heads = [slice(hh * HG_DIM, (hh + 1) * HG_DIM) for hh in range(HG_HEADS)]
    early_tiles = min(n_tiles, (hg_w + d_model) // GATE_TILE)
    for i in range(early_tiles):
        gate_tile(i)
    y_a = _dot((cb * y_conv).astype(BF16), wco_ref[...])
    ops = [chunk_operands(c) for c in range(n_chunks)]
    pair = [[lax.dot_general(op["qp"][:, sl], op["k_cat"][:, sl], _NT, preferred_element_type=F32) for sl in heads]
            for op in ops]
    upd = [[lax.dot_general(op["v"][:, sl], op["k_st"][:, sl], _TN, preferred_element_type=F32) for sl in heads]
           for op in ops]
    for i in range(early_tiles, n_tiles):
        gate_tile(i)
    chunks_per_seq = tq // chunk
    st_in = []
    for s in range(seqs):
        st = [st_scr[s * HG_HEADS + hh] for hh in range(HG_HEADS)]
        for c in range(s * chunks_per_seq, (s + 1) * chunks_per_seq):
            st_in.append([m.astype(BF16) for m in st])
            st = [ops[c]["dec"][:, sl] * st[hh] + upd[c][hh] for hh, sl in enumerate(heads)]
        for hh in range(HG_HEADS):
            st_scr[s * HG_HEADS + hh] = st[hh]
    for c, op in enumerate(ops):
        for hh, sl in enumerate(heads):
            w = jnp.where(mask, pair[c][hh], 0.0).astype(BF16)
            o_scr[c * chunk:(c + 1) * chunk, sl] = (
                _dot(w, op["v_cat"][:, sl])
                + lax.dot_general(op["q_in"][:, sl], st_in[c][hh], _NT, preferred_element_type=F32))

    n_hg = hg_w // GATE_TILE
    n_ga = d_model // GATE_TILE
    o_all = o_scr[...]
    hgn = hgn_ref[...]
    normed = []
    for hh in range(HG_HEADS):
        normed.append(_rms(o_all[:, hh * HG_DIM:(hh + 1) * HG_DIM], hgn))
    o_n = jnp.concatenate(normed, axis=1) * jnp.concatenate(gate_out[:n_hg], axis=1)
    y_b = _dot(o_n.astype(BF16), who_ref[...])
    merged = (jnp.concatenate(gate_out[n_hg:n_hg + n_ga], axis=1) * y_a
              + jnp.concatenate(gate_out[n_hg + n_ga:], axis=1) * y_b)
    x1_ref[...] = (x + _dot(merged.astype(BF16), wo_ref[...])).reshape(seqs, tq, d_model)

    @pl.when(t_idx == pl.num_programs(1) - 1)
    def _store_state():
        for s in range(seqs):
            for hh in range(HG_HEADS):
                sout_ref[s, hh] = st_scr[s * HG_HEADS + hh].T


def _xffn_kernel(x_ref, mk_ref, mv_ref, nx_ref, wq_ref, wxo_ref, nf_ref, wup_ref, wdn_ref, nfin_ref,
                 y_ref, mk_scr, mv_scr, *, final_norm):
    seqs, tq, d_model = x_ref.shape
    d_ff = wup_ref.shape[1]
    xd = d_model // X_HEADS

    @pl.when(pl.program_id(1) == 0)
    def _load_memory():
        mk_scr[...] = mk_ref[...].astype(BF16)
        mv_scr[...] = mv_ref[...].astype(BF16)

    x = x_ref[...].reshape(seqs * tq, d_model)
    h = _rms(x, nx_ref[...]).astype(BF16)
    q = _dot(h, wq_ref[...])
    heads = [slice(hh * xd, (hh + 1) * xd) for hh in range(X_HEADS)]
    q_b = q.astype(BF16)
    scores = [[lax.dot_general(q_b[s * tq:(s + 1) * tq, sl], mk_scr[s, :, sl], _NT, preferred_element_type=F32)
               * (xd ** -0.5) for sl in heads] for s in range(seqs)]
    probs = []
    for per_seq in scores:
        probs.append([])
        for sc in per_seq:
            e = jnp.exp(sc - jnp.max(sc, axis=-1, keepdims=True))
            probs[-1].append((e / jnp.sum(e, axis=-1, keepdims=True)).astype(BF16))
    o_rows = [jnp.concatenate([_dot(p, mv_scr[s, :, sl]) for p, sl in zip(probs[s], heads)], axis=1)
              for s in range(seqs)]
    o = o_rows[0] if seqs == 1 else jnp.concatenate(o_rows, axis=0)
    x2 = x + _dot(o.astype(BF16), wxo_ref[...])

    h3 = _rms(x2, nf_ref[...]).astype(BF16)
    acc = None
    for j in range(d_ff // FF_TILE):
        act = jnp.maximum(_dot(h3, wup_ref[:, j * FF_TILE:(j + 1) * FF_TILE]), 0.0)
        part = _dot((act * act).astype(BF16), wdn_ref[j * FF_TILE:(j + 1) * FF_TILE, :])
        acc = part if acc is None else acc + part
    x3 = x2 + acc
    y = _rms(x3, nfin_ref[...]) if final_norm else x3
    y_ref[...] = y.reshape(seqs, tq, d_model)


def _memkv_kernel(mem_ref, g_ref, wk_ref, wv_ref, mk_ref, mv_ref):
    mh = _rms(mem_ref[...], g_ref[...]).astype(BF16)
    mk_ref[...] = _dot(mh, wk_ref[...])
    mv_ref[...] = _dot(mh, wv_ref[...])


def _resident(shape):
    return pl.BlockSpec(shape, lambda *_: (0,) * len(shape), pipeline_mode=pl.Buffered(1))


def _token_tile(t_len, largest=512):
    for tm in (1024, 512, 256, 128, 64, 32, 16):
        if tm <= largest and t_len % tm == 0:
            return tm
    raise ValueError(f"sequence length {t_len} must be a multiple of {DECAY_BLOCK}")


def _params():
    return pltpu.CompilerParams(dimension_semantics=("arbitrary", "arbitrary"), vmem_limit_bytes=VMEM_LIMIT_BYTES)


def _tiling(b, t_len, largest, max_seqs):
    if t_len > HG_CHUNK:
        return 1, _token_tile(t_len, largest)
    seqs = max(s for s in range(1, b + 1) if b % s == 0 and s <= max_seqs and s * t_len <= largest)
    return seqs, t_len


def _mixer(x, prefix, s0, hg_lb, norm_mix, w_in, conv_w, hg_norm, w_conv_out, w_hg_out, w_o, *, layer):
    b, t_len, d = x.shape
    d_conv = conv_w.shape[1]
    hg_w = HG_HEADS * HG_DIM
    seqs, tq = _tiling(b, t_len, largest=1024, max_seqs=MIXER_SEQS_PER_TILE)
    tm = seqs * tq
    chunk = min(HG_CHUNK, tq)
    tok = pl.BlockSpec((seqs, tq, d), lambda i, j: (i, j, 0))
    per_batch = lambda *shape: pl.BlockSpec((seqs,) + shape, lambda i, j: (i,) + (0,) * len(shape))
    return pl.pallas_call(
        functools.partial(_mixer_kernel, layer=layer, seqs=seqs, tq=tq, chunk=chunk),
        grid=(b // seqs, t_len // tq),
        in_specs=[tok, per_batch(CONV_W - 1, d_conv), per_batch(HG_HEADS, HG_DIM, HG_DIM),
                  _resident(hg_lb.shape), _resident((1, d)), _resident(w_in.shape), _resident(conv_w.shape),
                  _resident((1, HG_DIM)), _resident(w_conv_out.shape), _resident(w_hg_out.shape),
                  _resident(w_o.shape)],
        out_specs=[tok, per_batch(CONV_W - 1, d_conv), per_batch(HG_HEADS, HG_DIM, HG_DIM)],
        out_shape=[jax.ShapeDtypeStruct((b, t_len, d), F32),
                   jax.ShapeDtypeStruct((b, CONV_W - 1, d_conv), F32),
                   jax.ShapeDtypeStruct((b, HG_HEADS, HG_DIM, HG_DIM), F32)],
        scratch_shapes=[pltpu.VMEM((seqs * HG_HEADS, HG_DIM, HG_DIM), F32),
                        pltpu.VMEM((seqs * (CONV_PAD + tq), d_conv), F32),
                        pltpu.VMEM((tm, hg_w), F32), pltpu.VMEM((tm, hg_w), F32),
                        pltpu.VMEM((tm, hg_w), BF16), pltpu.VMEM((tm, hg_w), F32),
                        pltpu.VMEM((tm, hg_w), F32)],
        compiler_params=_params(),
        name="mixer",
    )(x, prefix, s0, hg_lb, norm_mix.reshape(1, d), w_in, conv_w, hg_norm.reshape(1, HG_DIM),
      w_conv_out, w_hg_out, w_o)


def _xffn(x, mk, mv, norm_x, w_xq, w_xo, norm_ffn, w_up, w_down, norm_final, *, final_norm):
    b, t_len, d = x.shape
    n_mem = mk.shape[1]
    seqs, tq = _tiling(b, t_len, largest=1024, max_seqs=XFFN_SEQS_PER_TILE)
    tok = pl.BlockSpec((seqs, tq, d), lambda i, j: (i, j, 0))
    mem = pl.BlockSpec((seqs, n_mem, d), lambda i, j: (i, 0, 0))
    return pl.pallas_call(
        functools.partial(_xffn_kernel, final_norm=final_norm),
        grid=(b // seqs, t_len // tq),
        in_specs=[tok, mem, mem, _resident((1, d)), _resident(w_xq.shape), _resident(w_xo.shape),
                  _resident((1, d)), _resident(w_up.shape), _resident(w_down.shape), _resident((1, d))],
        out_specs=tok,
        out_shape=jax.ShapeDtypeStruct((b, t_len, d), F32),
        scratch_shapes=[pltpu.VMEM((seqs, n_mem, d), BF16), pltpu.VMEM((seqs, n_mem, d), BF16)],
        compiler_params=_params(),
        name="xffn",
    )(x, mk, mv, norm_x.reshape(1, d), w_xq, w_xo, norm_ffn.reshape(1, d), w_up, w_down, norm_final.reshape(1, d))


def _mem_kv(mem, g, w_xk, w_xv):
    b, n_mem, d = mem.shape
    blk = pl.BlockSpec((None, n_mem, d), lambda i: (i, 0, 0))
    return pl.pallas_call(
        _memkv_kernel,
        grid=(b,),
        in_specs=[blk, _resident((1, d)), _resident(w_xk.shape), _resident(w_xv.shape)],
        out_specs=[blk, blk],
        out_shape=[jax.ShapeDtypeStruct((b, n_mem, d), F32)] * 2,
        compiler_params=pltpu.CompilerParams(dimension_semantics=("arbitrary",), vmem_limit_bytes=VMEM_LIMIT_BYTES),
        name="mem_kv",
    )(mem, g.reshape(1, d), w_xk, w_xv)


def kernel(x_prompt, x_sample, mem_prompt, state_conv, state_hgrn, cache_mem_k, cache_mem_v, norm_mix, w_in, conv_w, hg_lb, hg_norm, w_conv_out, w_hg_out, w_o, norm_x, norm_mem, w_xq, w_xk, w_xv, w_xo, norm_ffn, w_up, w_down, norm_final):
    depth = w_in.shape[0]
    bp, _, d = x_prompt.shape
    bs = x_sample.shape[0]
    n_mem = mem_prompt.shape[1]
    d_conv = conv_w.shape[2]
    xd = d // X_HEADS
    hg_lb = hg_lb.astype(F32)
    xp, xs = x_prompt, x_sample
    conv_p, hg_p, mk_l, mv_l, conv_s, hg_s = [], [], [], [], [], []
    for l in range(depth):
        wb = lambda w: w[l].astype(BF16)
        mix_w = (hg_lb, norm_mix[l], wb(w_in), conv_w[l], hg_norm[l], wb(w_conv_out), wb(w_hg_out), wb(w_o))
        ffn_w = (norm_x[l], wb(w_xq), wb(w_xo), norm_ffn[l], wb(w_up), wb(w_down), norm_final)
        last = l == depth - 1
        mk, mv = _mem_kv(mem_prompt, norm_mem[l], wb(w_xk), wb(w_xv))
        xp, cst, st = _mixer(xp, jnp.zeros((bp, CONV_W - 1, d_conv), F32),
                             jnp.zeros((bp, HG_HEADS, HG_DIM, HG_DIM), F32), *mix_w, layer=l)
        xp = _xffn(xp, mk, mv, *ffn_w, final_norm=last)
        conv_p.append(cst)
        hg_p.append(st)
        mk_l.append(mk.reshape(bp, n_mem, X_HEADS, xd))
        mv_l.append(mv.reshape(bp, n_mem, X_HEADS, xd))
        xs, cst, st = _mixer(xs, state_conv[l], state_hgrn[l], *mix_w, layer=l)
        xs = _xffn(xs, cache_mem_k[l].reshape(bs, n_mem, d), cache_mem_v[l].reshape(bs, n_mem, d), *ffn_w,
                   final_norm=last)
        conv_s.append(cst)
        hg_s.append(st)
    return (xp, xs, jnp.stack(conv_p), jnp.stack(hg_p), jnp.stack(mk_l), jnp.stack(mv_l),
            jnp.stack(conv_s), jnp.stack(hg_s))
```

```python
import functools

import jax
import jax.numpy as jnp
from jax import lax
from jax.experimental import pallas as pl
from jax.experimental.pallas import tpu as pltpu

F32 = jnp.float32
BF16 = jnp.bfloat16
EPS = 1e-6

CONV_W = 3
HG_HEADS = 4
HG_DIM = 128
X_HEADS = 4
DECAY_BLOCK = 16
HG_CHUNK = 64
CONV_PAD = 8
FF_TILE = 1024
XFFN_MIN_GROUP_ROWS = 256
GATE_TILE = 256
MIXER_SEQS_PER_TILE = 16
XFFN_SEQS_PER_TILE = 4

V7X_VMEM_BYTES = 64 * 1024 * 1024
VMEM_LIMIT_BYTES = V7X_VMEM_BYTES - 8 * 1024 * 1024

_NT = (((1,), (1,)), ((), ()))
_TN = (((0,), (0,)), ((), ()))


def _rms(x, g):
    return x * lax.rsqrt(jnp.mean(x * x, axis=-1, keepdims=True) + EPS) * g


def _dot(a, b):
    return jnp.dot(a, b, preferred_element_type=F32)


def _cross_block_pieces(lo, hi):
    if hi - lo <= 1:
        return []
    mid = (lo + hi) // 2
    return ([(i, list(range(lo, mid))) for i in range(mid, hi)]
            + _cross_block_pieces(lo, mid) + _cross_block_pieces(mid, hi))


def _pair_mask(chunk, pieces):
    n_cols = chunk + DECAY_BLOCK * sum(len(js) for _, js in pieces)
    t = lax.broadcasted_iota(jnp.int32, (chunk, n_cols), 0)
    c = lax.broadcasted_iota(jnp.int32, (chunk, n_cols), 1)
    blk_shift = DECAY_BLOCK.bit_length() - 1
    t_blk = t >> blk_shift
    mask = (c < chunk) & ((c >> blk_shift) == t_blk) & (c <= t)
    col = chunk
    for i, js in pieces:
        width = DECAY_BLOCK * len(js)
        mask = mask | ((c >= col) & (c < col + width) & (t_blk == i))
        col += width
    return mask


def _mixer_kernel(x_ref, pre_ref, s0_ref, hglb_ref, nmix_ref, win_ref, cw_ref, hgn_ref, wco_ref, who_ref, wo_ref,
                  x1_ref, cst_ref, sout_ref,
                  st_scr, ubuf, qp_scr, kn_scr, v_scr, a_scr, o_scr, *, layer, seqs, tq, chunk):
    d_conv = cw_ref.shape[1]
    hg_w = HG_HEADS * HG_DIM
    d_model = x_ref.shape[2]
    tm = seqs * tq
    t_idx = pl.program_id(1)
    tail = CONV_W - 1
    seq_pitch = CONV_PAD + tq

    @pl.when(t_idx == 0)
    def _load_state():
        for s in range(seqs):
            for hh in range(HG_HEADS):
                st_scr[s * HG_HEADS + hh] = s0_ref[s, hh].T
            ubuf[s * seq_pitch + CONV_PAD - tail:s * seq_pitch + CONV_PAD, :] = pre_ref[s]

    c0 = 3 * d_conv
    g0 = c0 + 3 * hg_w

    x = x_ref[...].reshape(tm, d_model)
    halves = [(0, tm)] if tm % (2 * DECAY_BLOCK) else [(0, tm // 2), (tm // 2, tm)]
    h_parts = [_rms(x[lo:hi, :], nmix_ref[...]).astype(BF16) for lo, hi in halves]
    ph_parts = [_dot(h_g, win_ref[:, c0:g0]) for h_g in h_parts]
    h = h_parts[0] if len(halves) == 1 else jnp.concatenate(h_parts, axis=0)
    ph = ph_parts[0] if len(halves) == 1 else jnp.concatenate(ph_parts, axis=0)
    pc = _dot(h, win_ref[:, 0:c0])

    hq = ph[:, 0:hg_w]
    hf = ph[:, hg_w:2 * hg_w]
    e_lb = jnp.exp(hglb_ref[...] - jnp.max(hglb_ref[...], axis=0, keepdims=True))
    lb = jnp.sum(e_lb[0:layer + 1], axis=0, keepdims=True) / jnp.sum(e_lb, axis=0, keepdims=True)
    q = hq * jax.nn.sigmoid(hq)
    f = lb + (1.0 - lb) * jax.nn.sigmoid(hf)
    a = jnp.log(f)
    row_in_blk = lax.broadcasted_iota(jnp.int32, (tm, hg_w), 0) & (DECAY_BLOCK - 1)
    shift = 1
    while shift < DECAY_BLOCK:
        a = a + jnp.where(row_in_blk >= shift, pltpu.roll(a, shift, 0), 0.0)
        shift *= 2
    qp_scr[...] = q * jnp.exp(a)
    kn_scr[...] = (1.0 - f) * jnp.exp(-a)
    v_scr[...] = ph[:, 2 * hg_w:3 * hg_w].astype(BF16)
    a_scr[...] = a

    cb = pc[:, 0:d_conv]
    u = pc[:, d_conv:2 * d_conv] * pc[:, 2 * d_conv:3 * d_conv]
    cw = cw_ref[...]
    y_parts = []
    for s in range(seqs):
        base = s * seq_pitch + CONV_PAD
        ubuf[base:base + tq, :] = u[s * tq:(s + 1) * tq, :]
        y_s = ubuf[base - tail:base - tail + tq, :] * cw[0:1]
        for j in range(1, CONV_W):
            y_s = y_s + ubuf[base - tail + j:base - tail + j + tq, :] * cw[j:j + 1]
        y_parts.append(y_s)
        u_tail = ubuf[base + tq - tail:base + tq, :]
        ubuf[base - tail:base, :] = u_tail
        cst_ref[s] = u_tail
    y_conv = y_parts[0] if seqs == 1 else jnp.concatenate(y_parts, axis=0)

    gate_cols = hg_w + 2 * d_model
    gate_out = []

    def gate_tile(i):
        lo = i * GATE_TILE
        g = _dot(h, win_ref[:, g0 + lo:g0 + lo + GATE_TILE])
        gate_out.append(g * jax.nn.sigmoid(g) if lo < hg_w else jax.nn.sigmoid(g))

    nblk = chunk // DECAY_BLOCK
    pieces = _cross_block_pieces(0, nblk)
    mask = _pair_mask(chunk, pieces)

    def blk(arr, j):
        return arr[j * DECAY_BLOCK:(j + 1) * DECAY_BLOCK, :]

    def chunk_operands(c):
        r0 = c * chunk
        qp = qp_scr[r0:r0 + chunk, :]
        kn = kn_scr[r0:r0 + chunk, :]
        v = v_scr[r0:r0 + chunk, :]
        cs = [jnp.zeros((1, hg_w), F32)]
        for j in range(nblk):
            row = r0 + (j + 1) * DECAY_BLOCK - 1
            cs.append(cs[j] + a_scr[row:row + 1, :])
        k_parts, v_parts = [kn], [v]
        for i, js in pieces:
            for j in js:
                k_parts.append(blk(kn, j) * jnp.exp(cs[i] - cs[j]))
                v_parts.append(blk(v, j))
        return dict(
            qp=qp.astype(BF16), v=v, dec=jnp.exp(cs[nblk]),
            k_cat=jnp.concatenate(k_parts, axis=0).astype(BF16),
            v_cat=jnp.concatenate(v_parts, axis=0),
            q_in=jnp.concatenate([blk(qp, j) if j == 0 else blk(qp, j) * jnp.exp(cs[j]) for j in range(nblk)],
                                 axis=0).astype(BF16),
            k_st=jnp.concatenate([blk(kn, j) * jnp.exp(cs[nblk] - cs[j]) for j in range(nblk)],
                                 axis=0).astype(BF16))

    n_chunks = tm // chunk
    n_tiles = gate_cols // GATE_TILE
    heads = [slice(hh * HG_DIM, (hh + 1) * HG_DIM) for hh in range(HG_HEADS)]
    early_tiles = min(n_tiles, (hg_w + d_model) // GATE_TILE)
    for i in range(early_tiles):
        gate_tile(i)
    y_a = _dot((cb * y_conv).astype(BF16), wco_ref[...])
    ops = [chunk_operands(c) for c in range(n_chunks)]
    pair = [[lax.dot_general(op["qp"][:, sl], op["k_cat"][:, sl], _NT, preferred_element_type=F32) for sl in heads]
            for op in ops]
    upd = [[lax.dot_general(op["v"][:, sl], op["k_st"][:, sl], _TN, preferred_element_type=F32) for sl in heads]
           for op in ops]
    for i in range(early_tiles, n_tiles):
        gate_tile(i)
    chunks_per_seq = tq // chunk
    st_in = []
    for s in range(seqs):
        st = [st_scr[s * HG_HEADS + hh] for hh in range(HG_HEADS)]
        for c in range(s * chunks_per_seq, (s + 1) * chunks_per_seq):
            st_in.append([m.astype(BF16) for m in st])
            st = [ops[c]["dec"][:, sl] * st[hh] + upd[c][hh] for hh, sl in enumerate(heads)]
        for hh in range(HG_HEADS):
            st_scr[s * HG_HEADS + hh] = st[hh]
    for c, op in enumerate(ops):
        for hh, sl in enumerate(heads):
            w = jnp.where(mask, pair[c][hh], 0.0).astype(BF16)
            o_scr[c * chunk:(c + 1) * chunk, sl] = (
                _dot(w, op["v_cat"][:, sl])
                + lax.dot_general(op["q_in"][:, sl], st_in[c][hh], _NT, preferred_element_type=F32))

    n_hg = hg_w // GATE_TILE
    n_ga = d_model // GATE_TILE
    o_all = o_scr[...]
    hgn = hgn_ref[...]
    normed = []
    for hh in range(HG_HEADS):
        normed.append(_rms(o_all[:, hh * HG_DIM:(hh + 1) * HG_DIM], hgn))
    o_n = jnp.concatenate(normed, axis=1) * jnp.concatenate(gate_out[:n_hg], axis=1)
    y_b = _dot(o_n.astype(BF16), who_ref[...])
    merged = (jnp.concatenate(gate_out[n_hg:n_hg + n_ga], axis=1) * y_a
              + jnp.concatenate(gate_out[n_hg + n_ga:], axis=1) * y_b)
    merged_b = merged.astype(BF16)
    x1_parts = [x[lo:hi, :] + _dot(merged_b[lo:hi, :], wo_ref[...]) for lo, hi in halves]
    x1 = x1_parts[0] if len(halves) == 1 else jnp.concatenate(x1_parts, axis=0)
    x1_ref[...] = x1.reshape(seqs, tq, d_model)

    @pl.when(t_idx == pl.num_programs(1) - 1)
    def _store_state():
        for s in range(seqs):
            for hh in range(HG_HEADS):
                sout_ref[s, hh] = st_scr[s * HG_HEADS + hh].T


def _xffn_kernel(x_ref, mk_ref, mv_ref, nx_ref, wq_ref, wxo_ref, nf_ref, wup_ref, wdn_ref, nfin_ref,
                 y_ref, mk_scr, mv_scr, *, final_norm):
    seqs, tq, d_model = x_ref.shape
    d_ff = wup_ref.shape[1]
    xd = d_model // X_HEADS

    @pl.when(pl.program_id(1) == 0)
    def _load_memory():
        mk_scr[...] = mk_ref[...].astype(BF16)
        mv_scr[...] = mv_ref[...].astype(BF16)

    heads = [slice(hh * xd, (hh + 1) * xd) for hh in range(X_HEADS)]
    n_groups = 2 if seqs == 1 and tq % (2 * XFFN_MIN_GROUP_ROWS) == 0 else 1
    rows = seqs * tq // n_groups
    seq_parts = [(0, 0, rows)] if seqs == 1 else [(s, s * tq, tq) for s in range(seqs)]

    def load_x(g):
        return x_ref[0, g * rows:(g + 1) * rows, :] if seqs == 1 else x_ref[...].reshape(rows, d_model)

    def project_q(x):
        return _dot(_rms(x, nx_ref[...]).astype(BF16), wq_ref[...]).astype(BF16)

    def scores(q_b):
        return [[lax.dot_general(q_b[r0:r0 + n, sl], mk_scr[s, :, sl], _NT, preferred_element_type=F32) * (xd ** -0.5)
                 for sl in heads] for s, r0, n in seq_parts]

    def softmax(sc_all):
        out = []
        for per_seq in sc_all:
            out.append([])
            for sc in per_seq:
                e = jnp.exp(sc - jnp.max(sc, axis=-1, keepdims=True))
                out[-1].append((e / jnp.sum(e, axis=-1, keepdims=True)).astype(BF16))
        return out

    def attend(probs):
        o_rows = [jnp.concatenate([_dot(p, mv_scr[s, :, sl]) for p, sl in zip(probs[i], heads)], axis=1)
                  for i, (s, _, _) in enumerate(seq_parts)]
        return (o_rows[0] if len(o_rows) == 1 else jnp.concatenate(o_rows, axis=0)).astype(BF16)

    def mlp_and_store(g, x2):
        h3 = _rms(x2, nf_ref[...]).astype(BF16)
        acc = None
        for j in range(d_ff // FF_TILE):
            act = jnp.maximum(_dot(h3, wup_ref[:, j * FF_TILE:(j + 1) * FF_TILE]), 0.0)
            part = _dot((act * act).astype(BF16), wdn_ref[j * FF_TILE:(j + 1) * FF_TILE, :])
            acc = part if acc is None else acc + part
        x3 = x2 + acc
        y = _rms(x3, nfin_ref[...]) if final_norm else x3
        if seqs == 1:
            y_ref[0, g * rows:(g + 1) * rows, :] = y
        else:
            y_ref[...] = y.reshape(seqs, tq, d_model)

    xs = [load_x(g) for g in range(n_groups)]
    if n_groups == 1:
        o = attend(softmax(scores(project_q(xs[0]))))
        mlp_and_store(0, xs[0] + _dot(o, wxo_ref[...]))
    else:
        sc0 = scores(project_q(xs[0]))
        q1 = project_q(xs[1])
        o0 = attend(softmax(sc0))
        sc1 = scores(q1)
        x2_0 = xs[0] + _dot(o0, wxo_ref[...])
        o1 = attend(softmax(sc1))
        x2_1 = xs[1] + _dot(o1, wxo_ref[...])
        mlp_and_store(0, x2_0)
        mlp_and_store(1, x2_1)


def _memkv_kernel(mem_ref, g_ref, wk_ref, wv_ref, mk_ref, mv_ref):
    mh = _rms(mem_ref[...], g_ref[...]).astype(BF16)
    mk_ref[...] = _dot(mh, wk_ref[...])
    mv_ref[...] = _dot(mh, wv_ref[...])


def _resident(shape):
    return pl.BlockSpec(shape, lambda *_: (0,) * len(shape), pipeline_mode=pl.Buffered(1))


def _token_tile(t_len, largest=512):
    for tm in (1024, 512, 256, 128, 64, 32, 16):
        if tm <= largest and t_len % tm == 0:
            return tm
    raise ValueError(f"sequence length {t_len} must be a multiple of {DECAY_BLOCK}")


def _params():
    return pltpu.CompilerParams(dimension_semantics=("arbitrary", "arbitrary"), vmem_limit_bytes=VMEM_LIMIT_BYTES)


def _tiling(b, t_len, largest, max_seqs):
    if t_len > HG_CHUNK:
        return 1, _token_tile(t_len, largest)
    seqs = max(s for s in range(1, b + 1) if b % s == 0 and s <= max_seqs and s * t_len <= largest)
    return seqs, t_len


def _mixer(x, prefix, s0, hg_lb, norm_mix, w_in, conv_w, hg_norm, w_conv_out, w_hg_out, w_o, *, layer):
    b, t_len, d = x.shape
    d_conv = conv_w.shape[1]
    hg_w = HG_HEADS * HG_DIM
    seqs, tq = _tiling(b, t_len, largest=1024, max_seqs=MIXER_SEQS_PER_TILE)
    tm = seqs * tq
    chunk = min(HG_CHUNK, tq)
    tok = pl.BlockSpec((seqs, tq, d), lambda i, j: (i, j, 0))
    per_batch = lambda *shape: pl.BlockSpec((seqs,) + shape, lambda i, j: (i,) + (0,) * len(shape))
    return pl.pallas_call(
        functools.partial(_mixer_kernel, layer=layer, seqs=seqs, tq=tq, chunk=chunk),
        grid=(b // seqs, t_len // tq),
        in_specs=[tok, per_batch(CONV_W - 1, d_conv), per_batch(HG_HEADS, HG_DIM, HG_DIM),
                  _resident(hg_lb.shape), _resident((1, d)), _resident(w_in.shape), _resident(conv_w.shape),
                  _resident((1, HG_DIM)), _resident(w_conv_out.shape), _resident(w_hg_out.shape),
                  _resident(w_o.shape)],
        out_specs=[tok, per_batch(CONV_W - 1, d_conv), per_batch(HG_HEADS, HG_DIM, HG_DIM)],
        out_shape=[jax.ShapeDtypeStruct((b, t_len, d), F32),
                   jax.ShapeDtypeStruct((b, CONV_W - 1, d_conv), F32),
                   jax.ShapeDtypeStruct((b, HG_HEADS, HG_DIM, HG_DIM), F32)],
        scratch_shapes=[pltpu.VMEM((seqs * HG_HEADS, HG_DIM, HG_DIM), F32),
                        pltpu.VMEM((seqs * (CONV_PAD + tq), d_conv), F32),
                        pltpu.VMEM((tm, hg_w), F32), pltpu.VMEM((tm, hg_w), F32),
                        pltpu.VMEM((tm, hg_w), BF16), pltpu.VMEM((tm, hg_w), F32),
                        pltpu.VMEM((tm, hg_w), F32)],
        compiler_params=_params(),
        name="mixer",
    )(x, prefix, s0, hg_lb, norm_mix.reshape(1, d), w_in, conv_w, hg_norm.reshape(1, HG_DIM),
      w_conv_out, w_hg_out, w_o)


def _xffn(x, mk, mv, norm_x, w_xq, w_xo, norm_ffn, w_up, w_down, norm_final, *, final_norm):
    b, t_len, d = x.shape
    n_mem = mk.shape[1]
    seqs, tq = _tiling(b, t_len, largest=1024, max_seqs=XFFN_SEQS_PER_TILE)
    tok = pl.BlockSpec((seqs, tq, d), lambda i, j: (i, j, 0))
    mem = pl.BlockSpec((seqs, n_mem, d), lambda i, j: (i, 0, 0))
    return pl.pallas_call(
        functools.partial(_xffn_kernel, final_norm=final_norm),
        grid=(b // seqs, t_len // tq),
        in_specs=[tok, mem, mem, _resident((1, d)), _resident(w_xq.shape), _resident(w_xo.shape),
                  _resident((1, d)), _resident(w_up.shape), _resident(w_down.shape), _resident((1, d))],
        out_specs=tok,
        out_shape=jax.ShapeDtypeStruct((b, t_len, d), F32),
        scratch_shapes=[pltpu.VMEM((seqs, n_mem, d), BF16), pltpu.VMEM((seqs, n_mem, d), BF16)],
        compiler_params=_params(),
        name="xffn",
    )(x, mk, mv, norm_x.reshape(1, d), w_xq, w_xo, norm_ffn.reshape(1, d), w_up, w_down, norm_final.reshape(1, d))


def _mem_kv(mem, g, w_xk, w_xv):
    b, n_mem, d = mem.shape
    blk = pl.BlockSpec((None, n_mem, d), lambda i: (i, 0, 0))
    return pl.pallas_call(
        _memkv_kernel,
        grid=(b,),
        in_specs=[blk, _resident((1, d)), _resident(w_xk.shape), _resident(w_xv.shape)],
        out_specs=[blk, blk],
        out_shape=[jax.ShapeDtypeStruct((b, n_mem, d), F32)] * 2,
        compiler_params=pltpu.CompilerParams(dimension_semantics=("arbitrary",), vmem_limit_bytes=VMEM_LIMIT_BYTES),
        name="mem_kv",
    )(mem, g.reshape(1, d), w_xk, w_xv)


def kernel(x_prompt, x_sample, mem_prompt, state_conv, state_hgrn, cache_mem_k, cache_mem_v, norm_mix, w_in, conv_w, hg_lb, hg_norm, w_conv_out, w_hg_out, w_o, norm_x, norm_mem, w_xq, w_xk, w_xv, w_xo, norm_ffn, w_up, w_down, norm_final):
    depth = w_in.shape[0]
    bp, _, d = x_prompt.shape
    bs = x_sample.shape[0]
    n_mem = mem_prompt.shape[1]
    d_conv = conv_w.shape[2]
    xd = d // X_HEADS
    hg_lb = hg_lb.astype(F32)
    xp, xs = x_prompt, x_sample
    conv_p, hg_p, mk_l, mv_l, conv_s, hg_s = [], [], [], [], [], []
    for l in range(depth):
        wb = lambda w: w[l].astype(BF16)
        mix_w = (hg_lb, norm_mix[l], wb(w_in), conv_w[l], hg_norm[l], wb(w_conv_out), wb(w_hg_out), wb(w_o))
        ffn_w = (norm_x[l], wb(w_xq), wb(w_xo), norm_ffn[l], wb(w_up), wb(w_down), norm_final)
        last = l == depth - 1
        mk, mv = _mem_kv(mem_prompt, norm_mem[l], wb(w_xk), wb(w_xv))
        xp, cst, st = _mixer(xp, jnp.zeros((bp, CONV_W - 1, d_conv), F32),
                             jnp.zeros((bp, HG_HEADS, HG_DIM, HG_DIM), F32), *mix_w, layer=l)
        xp = _xffn(xp, mk, mv, *ffn_w, final_norm=last)
        conv_p.append(cst)
        hg_p.append(st)
        mk_l.append(mk.reshape(bp, n_mem, X_HEADS, xd))
        mv_l.append(mv.reshape(bp, n_mem, X_HEADS, xd))
        xs, cst, st = _mixer(xs, state_conv[l], state_hgrn[l], *mix_w, layer=l)
        xs = _xffn(xs, cache_mem_k[l].reshape(bs, n_mem, d), cache_mem_v[l].reshape(bs, n_mem, d), *ffn_w,
                   final_norm=last)
        conv_s.append(cst)
        hg_s.append(st)
    return (xp, xs, jnp.stack(conv_p), jnp.stack(hg_p), jnp.stack(mk_l), jnp.stack(mv_l),
            jnp.stack(conv_s), jnp.stack(hg_s))
```

```python
import functools

import jax
import jax.numpy as jnp
from jax import lax
from jax.experimental import pallas as pl
from jax.experimental.pallas import tpu as pltpu

F32 = jnp.float32
BF16 = jnp.bfloat16
EPS = 1e-6

CONV_W = 3
HG_HEADS = 4
HG_DIM = 128
X_HEADS = 4
DECAY_BLOCK = 16
HG_CHUNK = 64
HG_ISSUE_LEAD = 4
CONV_PAD = 8
FF_TILE = 1024
XFFN_MIN_GROUP_ROWS = 256
GATE_TILE = 256
MIXER_SEQS_PER_TILE = 16
XFFN_SEQS_PER_TILE = 4

V7X_VMEM_BYTES = 64 * 1024 * 1024
VMEM_LIMIT_BYTES = V7X_VMEM_BYTES - 8 * 1024 * 1024

_NT = (((1,), (1,)), ((), ()))
_TN = (((0,), (0,)), ((), ()))


def _rms(x, g):
    return x * lax.rsqrt(jnp.mean(x * x, axis=-1, keepdims=True) + EPS) * g


def _sigmoid(x):
    return 0.5 * jnp.tanh(0.5 * x) + 0.5


def _dot(a, b):
    return jnp.dot(a, b, preferred_element_type=F32)


def _cross_block_pieces(lo, hi):
    if hi - lo <= 1:
        return []
    mid = (lo + hi) // 2
    return ([(i, list(range(lo, mid))) for i in range(mid, hi)]
            + _cross_block_pieces(lo, mid) + _cross_block_pieces(mid, hi))


def _pair_mask(chunk, pieces):
    n_cols = chunk + DECAY_BLOCK * sum(len(js) for _, js in pieces)
    t = lax.broadcasted_iota(jnp.int32, (chunk, n_cols), 0)
    c = lax.broadcasted_iota(jnp.int32, (chunk, n_cols), 1)
    blk_shift = DECAY_BLOCK.bit_length() - 1
    t_blk = t >> blk_shift
    mask = (c < chunk) & ((c >> blk_shift) == t_blk) & (c <= t)
    col = chunk
    for i, js in pieces:
        width = DECAY_BLOCK * len(js)
        mask = mask | ((c >= col) & (c < col + width) & (t_blk == i))
        col += width
    return mask


def _mixer_kernel(x_ref, pre_ref, s0_ref, hglb_ref, nmix_ref, win_ref, cw_ref, hgn_ref, wco_ref, who_ref, wo_ref,
                  x1_ref, cst_ref, sout_ref,
                  st_scr, ubuf, qp_scr, kn_scr, v_scr, a_scr, o_scr, *, layer, seqs, tq, chunk):
    d_conv = cw_ref.shape[1]
    hg_w = HG_HEADS * HG_DIM
    d_model = x_ref.shape[2]
    tm = seqs * tq
    t_idx = pl.program_id(1)
    tail = CONV_W - 1
    seq_pitch = CONV_PAD + tq

    @pl.when(t_idx == 0)
    def _load_state():
        for s in range(seqs):
            for hh in range(HG_HEADS):
                st_scr[s * HG_HEADS + hh] = s0_ref[s, hh].T
            ubuf[s * seq_pitch + CONV_PAD - tail:s * seq_pitch + CONV_PAD, :] = pre_ref[s]

    c0 = 3 * d_conv
    g0 = c0 + 3 * hg_w

    x = x_ref[...].reshape(tm, d_model)
    halves = [(0, tm)] if tm % (2 * DECAY_BLOCK) else [(0, tm // 2), (tm // 2, tm)]
    h_parts = [_rms(x[lo:hi, :], nmix_ref[...]).astype(BF16) for lo, hi in halves]
    ph_parts = [_dot(h_g, win_ref[:, c0:g0]) for h_g in h_parts]
    h = h_parts[0] if len(halves) == 1 else jnp.concatenate(h_parts, axis=0)
    ph = ph_parts[0] if len(halves) == 1 else jnp.concatenate(ph_parts, axis=0)
    pc = _dot(h, win_ref[:, 0:c0])

    hq = ph[:, 0:hg_w]
    hf = ph[:, hg_w:2 * hg_w]
    e_lb = jnp.exp(hglb_ref[...] - jnp.max(hglb_ref[...], axis=0, keepdims=True))
    lb = jnp.sum(e_lb[0:layer + 1], axis=0, keepdims=True) / jnp.sum(e_lb, axis=0, keepdims=True)
    q = hq * _sigmoid(hq)
    f = lb + (1.0 - lb) * _sigmoid(hf)
    a = jnp.log(f)
    row_in_blk = lax.broadcasted_iota(jnp.int32, (tm, hg_w), 0) & (DECAY_BLOCK - 1)
    shift = 1
    while shift < DECAY_BLOCK:
        a = a + jnp.where(row_in_blk >= shift, pltpu.roll(a, shift, 0), 0.0)
        shift *= 2
    qp_scr[...] = q * jnp.exp(a)
    kn_scr[...] = (1.0 - f) * jnp.exp(-a)
    v_scr[...] = ph[:, 2 * hg_w:3 * hg_w].astype(BF16)
    a_scr[...] = a

    cb = pc[:, 0:d_conv]
    u = pc[:, d_conv:2 * d_conv] * pc[:, 2 * d_conv:3 * d_conv]
    cw = cw_ref[...]
    y_parts = []
    for s in range(seqs):
        base = s * seq_pitch + CONV_PAD
        ubuf[base:base + tq, :] = u[s * tq:(s + 1) * tq, :]
        y_s = ubuf[base - tail:base - tail + tq, :] * cw[0:1]
        for j in range(1, CONV_W):
            y_s = y_s + ubuf[base - tail + j:base - tail + j + tq, :] * cw[j:j + 1]
        y_parts.append(y_s)
        u_tail = ubuf[base + tq - tail:base + tq, :]
        ubuf[base - tail:base, :] = u_tail
        cst_ref[s] = u_tail
    y_conv = y_parts[0] if seqs == 1 else jnp.concatenate(y_parts, axis=0)

    gate_cols = hg_w + 2 * d_model
    gate_out = []

    def gate_tile(i):
        lo = i * GATE_TILE
        g = _dot(h, win_ref[:, g0 + lo:g0 + lo + GATE_TILE])
        gate_out.append(g * _sigmoid(g) if lo < hg_w else _sigmoid(g))

    nblk = chunk // DECAY_BLOCK
    pieces = _cross_block_pieces(0, nblk)
    mask = _pair_mask(chunk, pieces)

    def blk(arr, j):
        return arr[j * DECAY_BLOCK:(j + 1) * DECAY_BLOCK, :]

    def chunk_operands(c):
        r0 = c * chunk
        qp = qp_scr[r0:r0 + chunk, :]
        kn = kn_scr[r0:r0 + chunk, :]
        v = v_scr[r0:r0 + chunk, :]
        cs = [jnp.zeros((1, hg_w), F32)]
        for j in range(nblk):
            row = r0 + (j + 1) * DECAY_BLOCK - 1
            cs.append(cs[j] + a_scr[row:row + 1, :])
        k_parts, v_parts = [kn], [v]
        for i, js in pieces:
            for j in js:
                k_parts.append(blk(kn, j) * jnp.exp(cs[i] - cs[j]))
                v_parts.append(blk(v, j))
        return dict(
            qp=qp.astype(BF16), v=v, dec=jnp.exp(cs[nblk]),
            k_cat=jnp.concatenate(k_parts, axis=0).astype(BF16),
            v_cat=jnp.concatenate(v_parts, axis=0),
            q_in=jnp.concatenate([blk(qp, j) if j == 0 else blk(qp, j) * jnp.exp(cs[j]) for j in range(nblk)],
                                 axis=0).astype(BF16),
            k_st=jnp.concatenate([blk(kn, j) * jnp.exp(cs[nblk] - cs[j]) for j in range(nblk)],
                                 axis=0).astype(BF16))

    n_chunks = tm // chunk
    n_tiles = gate_cols // GATE_TILE
    heads = [slice(hh * HG_DIM, (hh + 1) * HG_DIM) for hh in range(HG_HEADS)]
    early_tiles = min(n_tiles, (hg_w + d_model) // GATE_TILE)
    for i in range(early_tiles):
        gate_tile(i)
    y_a = _dot((cb * y_conv).astype(BF16), wco_ref[...])
    ops = [chunk_operands(c) for c in range(n_chunks)]
    chunks_per_seq = tq // chunk
    pair, upd = {}, {}
    state = {s: [st_scr[s * HG_HEADS + hh] for hh in range(HG_HEADS)] for s in range(seqs)}

    def issue_pair_and_update(c):
        op = ops[c]
        pair[c] = [lax.dot_general(op["qp"][:, sl], op["k_cat"][:, sl], _NT, preferred_element_type=F32)
                   for sl in heads]
        upd[c] = [lax.dot_general(op["v"][:, sl], op["k_st"][:, sl], _TN, preferred_element_type=F32)
                  for sl in heads]

    def issue_outputs(c):
        op = ops[c]
        s = c // chunks_per_seq
        st_in = [m.T.astype(BF16) for m in state[s]]
        state[s] = [op["dec"][:, sl] * state[s][hh] + upd[c][hh] for hh, sl in enumerate(heads)]
        for hh, sl in enumerate(heads):
            w = jnp.where(mask, pair[c][hh], 0.0).astype(BF16)
            o_scr[c * chunk:(c + 1) * chunk, sl] = (
                _dot(w, op["v_cat"][:, sl]) + _dot(op["q_in"][:, sl], st_in[hh]))

    lead = min(n_chunks, HG_ISSUE_LEAD)
    for c in range(lead):
        issue_pair_and_update(c)
    for c in range(n_chunks):
        if c + lead < n_chunks:
            issue_pair_and_update(c + lead)
        issue_outputs(c)
    for s in range(seqs):
        for hh in range(HG_HEADS):
            st_scr[s * HG_HEADS + hh] = state[s][hh]
    for i in range(early_tiles, n_tiles):
        gate_tile(i)

    n_hg = hg_w // GATE_TILE
    n_ga = d_model // GATE_TILE
    o_all = o_scr[...]
    hgn = hgn_ref[...]
    normed = []
    for hh in range(HG_HEADS):
        normed.append(_rms(o_all[:, hh * HG_DIM:(hh + 1) * HG_DIM], hgn))
    o_n = jnp.concatenate(normed, axis=1) * jnp.concatenate(gate_out[:n_hg], axis=1)
    y_b = _dot(o_n.astype(BF16), who_ref[...])
    merged = (jnp.concatenate(gate_out[n_hg:n_hg + n_ga], axis=1) * y_a
              + jnp.concatenate(gate_out[n_hg + n_ga:], axis=1) * y_b)
    merged_b = merged.astype(BF16)
    x1_parts = [x[lo:hi, :] + _dot(merged_b[lo:hi, :], wo_ref[...]) for lo, hi in halves]
    x1 = x1_parts[0] if len(halves) == 1 else jnp.concatenate(x1_parts, axis=0)
    x1_ref[...] = x1.reshape(seqs, tq, d_model)

    @pl.when(t_idx == pl.num_programs(1) - 1)
    def _store_state():
        for s in range(seqs):
            for hh in range(HG_HEADS):
                sout_ref[s, hh] = st_scr[s * HG_HEADS + hh].T


def _xffn_kernel(x_ref, mk_ref, mv_ref, nx_ref, wq_ref, wxo_ref, nf_ref, wup_ref, wdn_ref, nfin_ref,
                 y_ref, mk_scr, mv_scr, *, final_norm):
    seqs, tq, d_model = x_ref.shape
    d_ff = wup_ref.shape[1]
    xd = d_model // X_HEADS

    @pl.when(pl.program_id(1) == 0)
    def _load_memory():
        mk_scr[...] = mk_ref[...].astype(BF16)
        mv_scr[...] = mv_ref[...].astype(BF16)

    heads = [slice(hh * xd, (hh + 1) * xd) for hh in range(X_HEADS)]
    n_groups = 2 if seqs == 1 and tq % (2 * XFFN_MIN_GROUP_ROWS) == 0 else 1
    rows = seqs * tq // n_groups
    seq_parts = [(0, 0, rows)] if seqs == 1 else [(s, s * tq, tq) for s in range(seqs)]

    def load_x(g):
        return x_ref[0, g * rows:(g + 1) * rows, :] if seqs == 1 else x_ref[...].reshape(rows, d_model)

    def project_q(x):
        return _dot(_rms(x, nx_ref[...]).astype(BF16), wq_ref[...]).astype(BF16)

    def scores(q_b):
        return [[lax.dot_general(q_b[r0:r0 + n, sl], mk_scr[s, :, sl], _NT, preferred_element_type=F32) * (xd ** -0.5)
                 for sl in heads] for s, r0, n in seq_parts]

    def softmax(sc_all):
        out = []
        for per_seq in sc_all:
            out.append([])
            for sc in per_seq:
                e = jnp.exp(sc - jnp.max(sc, axis=-1, keepdims=True))
                out[-1].append((e / jnp.sum(e, axis=-1, keepdims=True)).astype(BF16))
        return out

    def attend(probs):
        o_rows = [jnp.concatenate([_dot(p, mv_scr[s, :, sl]) for p, sl in zip(probs[i], heads)], axis=1)
                  for i, (s, _, _) in enumerate(seq_parts)]
        return (o_rows[0] if len(o_rows) == 1 else jnp.concatenate(o_rows, axis=0)).astype(BF16)

    def mlp_and_store(g, x2):
        h3 = _rms(x2, nf_ref[...]).astype(BF16)
        acc = None
        for j in range(d_ff // FF_TILE):
            act = jnp.maximum(_dot(h3, wup_ref[:, j * FF_TILE:(j + 1) * FF_TILE]), 0.0)
            part = _dot((act * act).astype(BF16), wdn_ref[j * FF_TILE:(j + 1) * FF_TILE, :])
            acc = part if acc is None else acc + part
        x3 = x2 + acc
        y = _rms(x3, nfin_ref[...]) if final_norm else x3
        if seqs == 1:
            y_ref[0, g * rows:(g + 1) * rows, :] = y
        else:
            y_ref[...] = y.reshape(seqs, tq, d_model)

    xs = [load_x(g) for g in range(n_groups)]
    if n_groups == 1:
        o = attend(softmax(scores(project_q(xs[0]))))
        mlp_and_store(0, xs[0] + _dot(o, wxo_ref[...]))
    else:
        sc0 = scores(project_q(xs[0]))
        q1 = project_q(xs[1])
        o0 = attend(softmax(sc0))
        sc1 = scores(q1)
        x2_0 = xs[0] + _dot(o0, wxo_ref[...])
        o1 = attend(softmax(sc1))
        x2_1 = xs[1] + _dot(o1, wxo_ref[...])
        mlp_and_store(0, x2_0)
        mlp_and_store(1, x2_1)


def _memkv_kernel(mem_ref, g_ref, wk_ref, wv_ref, mk_ref, mv_ref):
    mh = _rms(mem_ref[...], g_ref[...]).astype(BF16)
    mk_ref[...] = _dot(mh, wk_ref[...])
    mv_ref[...] = _dot(mh, wv_ref[...])


def _resident(shape):
    return pl.BlockSpec(shape, lambda *_: (0,) * len(shape), pipeline_mode=pl.Buffered(1))


def _token_tile(t_len, largest=512):
    for tm in (1024, 512, 256, 128, 64, 32, 16):
        if tm <= largest and t_len % tm == 0:
            return tm
    raise ValueError(f"sequence length {t_len} must be a multiple of {DECAY_BLOCK}")


def _params():
    return pltpu.CompilerParams(dimension_semantics=("arbitrary", "arbitrary"), vmem_limit_bytes=VMEM_LIMIT_BYTES)


def _tiling(b, t_len, largest, max_seqs):
    if t_len > HG_CHUNK:
        return 1, _token_tile(t_len, largest)
    seqs = max(s for s in range(1, b + 1) if b % s == 0 and s <= max_seqs and s * t_len <= largest)
    return seqs, t_len


def _mixer(x, prefix, s0, hg_lb, norm_mix, w_in, conv_w, hg_norm, w_conv_out, w_hg_out, w_o, *, layer):
    b, t_len, d = x.shape
    d_conv = conv_w.shape[1]
    hg_w = HG_HEADS * HG_DIM
    seqs, tq = _tiling(b, t_len, largest=1024, max_seqs=MIXER_SEQS_PER_TILE)
    tm = seqs * tq
    chunk = min(HG_CHUNK, tq)
    tok = pl.BlockSpec((seqs, tq, d), lambda i, j: (i, j, 0))
    per_batch = lambda *shape: pl.BlockSpec((seqs,) + shape, lambda i, j: (i,) + (0,) * len(shape))
    return pl.pallas_call(
        functools.partial(_mixer_kernel, layer=layer, seqs=seqs, tq=tq, chunk=chunk),
        grid=(b // seqs, t_len // tq),
        in_specs=[tok, per_batch(CONV_W - 1, d_conv), per_batch(HG_HEADS, HG_DIM, HG_DIM),
                  _resident(hg_lb.shape), _resident((1, d)), _resident(w_in.shape), _resident(conv_w.shape),
                  _resident((1, HG_DIM)), _resident(w_conv_out.shape), _resident(w_hg_out.shape),
                  _resident(w_o.shape)],
        out_specs=[tok, per_batch(CONV_W - 1, d_conv), per_batch(HG_HEADS, HG_DIM, HG_DIM)],
        out_shape=[jax.ShapeDtypeStruct((b, t_len, d), F32),
                   jax.ShapeDtypeStruct((b, CONV_W - 1, d_conv), F32),
                   jax.ShapeDtypeStruct((b, HG_HEADS, HG_DIM, HG_DIM), F32)],
        scratch_shapes=[pltpu.VMEM((seqs * HG_HEADS, HG_DIM, HG_DIM), F32),
                        pltpu.VMEM((seqs * (CONV_PAD + tq), d_conv), F32),
                        pltpu.VMEM((tm, hg_w), F32), pltpu.VMEM((tm, hg_w), F32),
                        pltpu.VMEM((tm, hg_w), BF16), pltpu.VMEM((tm, hg_w), F32),
                        pltpu.VMEM((tm, hg_w), F32)],
        compiler_params=_params(),
        name="mixer",
    )(x, prefix, s0, hg_lb, norm_mix.reshape(1, d), w_in, conv_w, hg_norm.reshape(1, HG_DIM),
      w_conv_out, w_hg_out, w_o)


def _xffn(x, mk, mv, norm_x, w_xq, w_xo, norm_ffn, w_up, w_down, norm_final, *, final_norm):
    b, t_len, d = x.shape
    n_mem = mk.shape[1]
    seqs, tq = _tiling(b, t_len, largest=1024, max_seqs=XFFN_SEQS_PER_TILE)
    tok = pl.BlockSpec((seqs, tq, d), lambda i, j: (i, j, 0))
    mem = pl.BlockSpec((seqs, n_mem, d), lambda i, j: (i, 0, 0))
    return pl.pallas_call(
        functools.partial(_xffn_kernel, final_norm=final_norm),
        grid=(b // seqs, t_len // tq),
        in_specs=[tok, mem, mem, _resident((1, d)), _resident(w_xq.shape), _resident(w_xo.shape),
                  _resident((1, d)), _resident(w_up.shape), _resident(w_down.shape), _resident((1, d))],
        out_specs=tok,
        out_shape=jax.ShapeDtypeStruct((b, t_len, d), F32),
        scratch_shapes=[pltpu.VMEM((seqs, n_mem, d), BF16), pltpu.VMEM((seqs, n_mem, d), BF16)],
        compiler_params=_params(),
        name="xffn",
    )(x, mk, mv, norm_x.reshape(1, d), w_xq, w_xo, norm_ffn.reshape(1, d), w_up, w_down, norm_final.reshape(1, d))


def _mem_kv(mem, g, w_xk, w_xv):
    b, n_mem, d = mem.shape
    blk = pl.BlockSpec((None, n_mem, d), lambda i: (i, 0, 0))
    return pl.pallas_call(
        _memkv_kernel,
        grid=(b,),
        in_specs=[blk, _resident((1, d)), _resident(w_xk.shape), _resident(w_xv.shape)],
        out_specs=[blk, blk],
        out_shape=[jax.ShapeDtypeStruct((b, n_mem, d), F32)] * 2,
        compiler_params=pltpu.CompilerParams(dimension_semantics=("arbitrary",), vmem_limit_bytes=VMEM_LIMIT_BYTES),
        name="mem_kv",
    )(mem, g.reshape(1, d), w_xk, w_xv)


def kernel(x_prompt, x_sample, mem_prompt, state_conv, state_hgrn, cache_mem_k, cache_mem_v, norm_mix, w_in, conv_w, hg_lb, hg_norm, w_conv_out, w_hg_out, w_o, norm_x, norm_mem, w_xq, w_xk, w_xv, w_xo, norm_ffn, w_up, w_down, norm_final):
    depth = w_in.shape[0]
    bp, _, d = x_prompt.shape
    bs = x_sample.shape[0]
    n_mem = mem_prompt.shape[1]
    d_conv = conv_w.shape[2]
    xd = d // X_HEADS
    hg_lb = hg_lb.astype(F32)
    xp, xs = x_prompt, x_sample
    conv_p, hg_p, mk_l, mv_l, conv_s, hg_s = [], [], [], [], [], []
    for l in range(depth):
        wb = lambda w: w[l].astype(BF16)
        mix_w = (hg_lb, norm_mix[l], wb(w_in), conv_w[l], hg_norm[l], wb(w_conv_out), wb(w_hg_out), wb(w_o))
        ffn_w = (norm_x[l], wb(w_xq), wb(w_xo), norm_ffn[l], wb(w_up), wb(w_down), norm_final)
        last = l == depth - 1
        mk, mv = _mem_kv(mem_prompt, norm_mem[l], wb(w_xk), wb(w_xv))
        xp, cst, st = _mixer(xp, jnp.zeros((bp, CONV_W - 1, d_conv), F32),
                             jnp.zeros((bp, HG_HEADS, HG_DIM, HG_DIM), F32), *mix_w, layer=l)
        xp = _xffn(xp, mk, mv, *ffn_w, final_norm=last)
        conv_p.append(cst)
        hg_p.append(st)
        mk_l.append(mk.reshape(bp, n_mem, X_HEADS, xd))
        mv_l.append(mv.reshape(bp, n_mem, X_HEADS, xd))
        xs, cst, st = _mixer(xs, state_conv[l], state_hgrn[l], *mix_w, layer=l)
        xs = _xffn(xs, cache_mem_k[l].reshape(bs, n_mem, d), cache_mem_v[l].reshape(bs, n_mem, d), *ffn_w,
                   final_norm=last)
        conv_s.append(cst)
        hg_s.append(st)
    return (xp, xs, jnp.stack(conv_p), jnp.stack(hg_p), jnp.stack(mk_l), jnp.stack(mv_l),
            jnp.stack(conv_s), jnp.stack(hg_s))
```
